```python
import math
import jax, jax.numpy as jnp
from jax import lax
import numpy as np

D_MODEL = 2048
BATCH = 4
SEQ = 4096
DEPTH = 1

CHUNK = 64
N_RET_HEADS = 8
RET_HEAD_DIM = D_MODEL // N_RET_HEADS
RET_V_HEAD_DIM = D_MODEL // N_RET_HEADS
D_RET = N_RET_HEADS * RET_HEAD_DIM
D_RET_V = N_RET_HEADS * RET_V_HEAD_DIM
POOL_WINDOWS = (2, 4, 8, 16)
POOL_GROUPS = len(POOL_WINDOWS)
D_POOL = D_MODEL // 2
POOL_GROUP_DIM = D_POOL // POOL_GROUPS
N_BRANCHES = 2
D_FF = ((8 * D_MODEL // 3 + 255) // 256) * 256
ROPE_BASE = 10000.0
NORM_EPS = 1e-6
PROJ_SIZES = (D_RET, D_RET, D_RET_V, D_RET_V, D_POOL, N_BRANCHES * D_MODEL)
D_PROJ = sum(PROJ_SIZES)

kernel_name = "hybrid_retention_pool_block"


def rms_norm(x, g):
    xf = x.astype(jnp.float32)
    y = xf * lax.rsqrt(jnp.mean(xf * xf, axis=-1, keepdims=True) + NORM_EPS)
    if g is not None:
        y = y * g.astype(jnp.float32)
    return y.astype(x.dtype)


def rotary(t, pos):
    dh = t.shape[-1]
    inv_freq = 1.0 / (ROPE_BASE ** (jnp.arange(0, dh, 2, dtype=jnp.float32) / dh))
    ang = pos.astype(jnp.float32)[:, None] * inv_freq[None, :]
    cos = jnp.cos(ang)[None, :, None, :]
    sin = jnp.sin(ang)[None, :, None, :]
    tf = t.astype(jnp.float32)
    t1, t2 = tf[..., : dh // 2], tf[..., dh // 2:]
    out = jnp.concatenate([t1 * cos - t2 * sin, t1 * sin + t2 * cos], axis=-1)
    return out.astype(t.dtype)


def retention_decays(dtype):
    log_g = jnp.log(1.0 - 2.0 ** (-5.0 - jnp.arange(N_RET_HEADS, dtype=jnp.float32)))
    n = jnp.arange(CHUNK, dtype=jnp.float32)
    dist = jnp.abs(n[:, None] - n[None, :])
    d_intra = jnp.exp(log_g[:, None, None] * dist[None])
    q_decay = jnp.exp(log_g[:, None] * (n[None, :] + 1.0))
    k_decay = jnp.exp(log_g[:, None] * (CHUNK - 1.0 - n[None, :]))
    chunk_decay = jnp.exp(log_g * CHUNK)
    return (d_intra.astype(dtype), q_decay.astype(dtype),
            k_decay.astype(dtype), chunk_decay.astype(dtype))


def chunkwise_retention(q, k, v):
    B, S, H, dk = q.shape
    dv = v.shape[-1]
    nc = S // CHUNK

    def to_chunks(t):
        d = t.shape[-1]
        return t.reshape(B, nc, CHUNK, H, d).transpose(1, 0, 3, 2, 4)

    qc, kc, vc = to_chunks(q), to_chunks(k), to_chunks(v)
    d_intra, q_decay, k_decay, chunk_decay = retention_decays(q.dtype)

    def step(state, xs):
        q_i, k_i, v_i = xs
        scores = jnp.einsum('bhnd,bhmd->bhnm', q_i, k_i) * d_intra[None]
        o_intra = jnp.einsum('bhnm,bhmv->bhnv', scores, v_i)
        o_cross = jnp.einsum('bhnd,bhdv->bhnv', q_i * q_decay[None, :, :, None], state)
        new_state = state * chunk_decay[None, :, None, None] + jnp.einsum(
            'bhmd,bhmv->bhdv', k_i * k_decay[None, :, :, None], v_i)
        return new_state, o_intra + o_cross

    state0 = jnp.zeros((B, H, dk, dv), dtype=q.dtype)
    _, o = lax.scan(step, state0, (qc, kc, vc))
    return o.transpose(1, 0, 3, 2, 4).reshape(B, S, H, dv)


def multiscale_causal_pool(p):
    B, S, G, Cg = p.shape
    pf = p.astype(jnp.float32)
    cs = jnp.concatenate([jnp.zeros((B, 1, G, Cg), jnp.float32),
                          jnp.cumsum(pf, axis=1)], axis=1)
    t = jnp.arange(S)
    outs = []
    for g, w in enumerate(POOL_WINDOWS):
        lo = jnp.maximum(t + 1 - w, 0)
        cnt = (t + 1 - lo).astype(jnp.float32)
        win_sum = cs[:, t + 1, g, :] - cs[:, lo, g, :]
        outs.append(win_sum / cnt[None, :, None])
    pooled = jnp.stack(outs, axis=2) - pf
    return pooled.astype(p.dtype)


def setup_inputs(seed: int = 0) -> dict:
    key = jax.random.key(seed)
    ks = jax.random.split(key, 13)
    f32 = jnp.float32

    def dense(k, shape, fan_in):
        return jax.random.normal(k, shape, f32) * (fan_in ** -0.5)

    def gain(k, shape):
        return 1.0 + 0.02 * jax.random.normal(k, shape, f32)

    return {
        "x": jax.random.normal(ks[0], (BATCH, SEQ, D_MODEL), f32),
        "norm1_g": gain(ks[1], (DEPTH, D_MODEL)),
        "w_in": dense(ks[2], (DEPTH, D_MODEL, D_PROJ), D_MODEL),
        "w_ret_branch": dense(ks[3], (DEPTH, D_RET_V, D_MODEL), D_RET_V),
        "w_pool_group": dense(ks[4], (DEPTH, POOL_GROUPS, POOL_GROUP_DIM, POOL_GROUP_DIM), POOL_GROUP_DIM),
        "pool_scale": gain(ks[5], (DEPTH, D_POOL)),
        "w_pool_branch": dense(ks[6], (DEPTH, D_POOL, D_MODEL), D_POOL),
        "w_out": dense(ks[7], (DEPTH, D_MODEL, D_MODEL), D_MODEL),
        "norm2_g": gain(ks[8], (DEPTH, D_MODEL)),
        "w_ffn_in": dense(ks[9], (DEPTH, D_MODEL, 2 * D_FF), D_MODEL),
        "w_ffn_out": dense(ks[10], (DEPTH, D_FF, D_MODEL), D_FF),
        "norm_final_g": gain(ks[11], (D_MODEL,)),
    }


def reference(x, norm1_g, w_in, w_ret_branch, w_pool_group, pool_scale, w_pool_branch,
              w_out, norm2_g, w_ffn_in, w_ffn_out, norm_final_g):
    B, S, _ = x.shape
    pos = jnp.arange(S)
    split_idx = list(np.cumsum(PROJ_SIZES)[:-1])
    scale = RET_HEAD_DIM ** -0.5
    h = x
    for l in range(DEPTH):
        u = rms_norm(h, norm1_g[l])
        proj = u @ w_in[l]
        q, k, v, rg, pz, gates = jnp.split(proj, split_idx, axis=-1)

        q = rotary(q.reshape(B, S, N_RET_HEADS, RET_HEAD_DIM), pos) * scale
        k = rotary(k.reshape(B, S, N_RET_HEADS, RET_HEAD_DIM), pos) * scale
        v = v.reshape(B, S, N_RET_HEADS, RET_V_HEAD_DIM)
        o = chunkwise_retention(q, k, v)
        o = rms_norm(o, None).reshape(B, S, D_RET_V)
        y_ret = (o * jax.nn.silu(rg)) @ w_ret_branch[l]

        pz = pz.reshape(B, S, POOL_GROUPS, POOL_GROUP_DIM)
        pooled = multiscale_causal_pool(pz)
        pooled = jnp.einsum('bsgc,gcd->bsgd', pooled, w_pool_group[l]).reshape(B, S, D_POOL)
        y_pool = (pooled * pool_scale[l]) @ w_pool_branch[l]

        g_ret, g_pool = jnp.split(gates, 2, axis=-1)
        merged = jax.nn.sigmoid(g_ret) * y_ret + jax.nn.sigmoid(g_pool) * y_pool
        h = h + merged @ w_out[l]

        u2 = rms_norm(h, norm2_g[l])
        a, b = jnp.split(u2 @ w_ffn_in[l], 2, axis=-1)
        h = h + (jax.nn.silu(a) * b) @ w_ffn_out[l]
    return rms_norm(h, norm_final_g)
```

```python
import functools
import math

import jax
import jax.numpy as jnp
import numpy as np
from jax import lax
from jax.experimental import pallas as pl
from jax.experimental.pallas import tpu as pltpu

D_MODEL = 2048
CHUNK = 64
N_HEADS = 8
HEAD_DIM = 256
D_RET = N_HEADS * HEAD_DIM
POOL_WINDOWS = (2, 4, 8, 16)
POOL_GROUP_DIM = 256
D_POOL = 1024
D_FF = 5632
D_PROJ = 4 * D_RET + D_POOL + 2 * D_MODEL
ROPE_BASE = 10000.0
NORM_EPS = 1e-6
POOL_HALO = 16

VMEM_LIMIT_BYTES = 56 * 1024 * 1024
RET_BLOCK = 256

F32 = jnp.float32
BF16 = jnp.bfloat16


def _rms_scale(xf):
    return xf * lax.rsqrt(jnp.mean(xf * xf, axis=-1, keepdims=True) + NORM_EPS)


def _sigmoid(x):
    return 1.0 / (1.0 + jnp.exp(-x))


def _norm_rows_to(dst_ref, src_ref, gain_ref, rows_per_chunk=256):
    n_chunks = src_ref.shape[0] // rows_per_chunk

    def body(c, carry):
        r0 = pl.multiple_of(c * rows_per_chunk, rows_per_chunk)
        xf = src_ref[pl.ds(r0, rows_per_chunk), :]
        dst_ref[pl.ds(r0, rows_per_chunk), :] = (_rms_scale(xf) * gain_ref[...]).astype(BF16)
        return carry

    lax.fori_loop(0, n_chunks, body, 0)


def _inproj_kernel(x_ref, g_ref, w_ref, cos_ref, sin_ref, wpg_ref, pscale_ref, o_ref,
                   u_ref, halo_ref, ext_ref, *, tm, tn, blocks_per_seq):
    i = pl.program_id(0)
    j = pl.program_id(1)
    heads_per_tile = tn // HEAD_DIM
    n_qk = 2 * D_RET // tn
    n_v = D_RET // tn
    j_v = n_qk
    j_rg = j_v + n_v
    j_pool = j_rg + n_v
    n_pool = D_POOL // tn
    j_gate = j_pool + n_pool

    @pl.when(j == 0)
    def _():
        _norm_rows_to(u_ref, x_ref, g_ref)

    acc = jnp.dot(u_ref[...], w_ref[...], preferred_element_type=F32)

    @pl.when(j < n_qk)
    def _():
        cos = cos_ref[...]
        sin = sin_ref[...]
        half = HEAD_DIM // 2
        for hh in range(heads_per_tile):
            c0 = hh * HEAD_DIM
            t1 = acc[:, c0:c0 + half]
            t2 = acc[:, c0 + half:c0 + HEAD_DIM]
            o_ref[:, c0:c0 + half] = (t1 * cos - t2 * sin).astype(BF16)
            o_ref[:, c0 + half:c0 + HEAD_DIM] = (t1 * sin + t2 * cos).astype(BF16)

    @pl.when(jnp.logical_and(j >= j_v, j < j_rg))
    def _():
        o_ref[...] = acc.astype(BF16)

    @pl.when(jnp.logical_and(j >= j_rg, j < j_pool))
    def _():
        o_ref[...] = (acc * _sigmoid(acc)).astype(BF16)

    @pl.when(j >= j_gate)
    def _():
        o_ref[...] = _sigmoid(acc).astype(BF16)

    seq_block = i % blocks_per_seq
    for jp in range(n_pool):
        @pl.when(j == j_pool + jp)
        def _(jp=jp):
            c_lo = jp * tn
            prev = halo_ref[:, c_lo:c_lo + tn]
            prev = jnp.where(seq_block == 0, jnp.zeros_like(prev), prev)
            ext_ref[0:POOL_HALO, :] = prev
            ext_ref[POOL_HALO:POOL_HALO + tm, :] = acc
            halo_ref[:, c_lo:c_lo + tn] = acc[tm - POOL_HALO:, :]
            pos = seq_block * tm + lax.broadcasted_iota(jnp.int32, (tm, 1), 0)
            for gg in range(tn // POOL_GROUP_DIM):
                grp = jp * (tn // POOL_GROUP_DIM) + gg
                win = POOL_WINDOWS[grp]
                cs = slice(gg * POOL_GROUP_DIM, (gg + 1) * POOL_GROUP_DIM)
                s = ext_ref[POOL_HALO:POOL_HALO + tm, cs]
                for d in range(1, win):
                    s = s + ext_ref[POOL_HALO - d:POOL_HALO - d + tm, cs]
                inv_cnt = 1.0 / jnp.minimum(pos + 1, win).astype(F32)
                pooled = s * inv_cnt - acc[:, cs]
                y = jnp.dot(pooled.astype(BF16), wpg_ref[grp], preferred_element_type=F32)
                gc = c_lo + gg * POOL_GROUP_DIM
                y = y * pscale_ref[:, gc:gc + POOL_GROUP_DIM]
                o_ref[:, cs] = y.astype(BF16)


def _in_projection(x2d, g1, w_in, cos_t, sin_t, wpg, pscale, *, seq, tm=1024, tn=512):
    m = x2d.shape[0]
    blocks_per_seq = seq // tm
    kern = functools.partial(_inproj_kernel, tm=tm, tn=tn, blocks_per_seq=blocks_per_seq)
    return pl.pallas_call(
        kern,
        grid=(m // tm, D_PROJ // tn),
        in_specs=[
            pl.BlockSpec((tm, D_MODEL), lambda i, j: (i, 0)),
            pl.BlockSpec((1, D_MODEL), lambda i, j: (0, 0)),
            pl.BlockSpec((D_MODEL, tn), lambda i, j: (0, j)),
            pl.BlockSpec((tm, HEAD_DIM // 2), lambda i, j: (i % blocks_per_seq, 0)),
            pl.BlockSpec((tm, HEAD_DIM // 2), lambda i, j: (i % blocks_per_seq, 0)),
            pl.BlockSpec((len(POOL_WINDOWS), POOL_GROUP_DIM, POOL_GROUP_DIM), lambda i, j: (0, 0, 0)),
            pl.BlockSpec((1, D_POOL), lambda i, j: (0, 0)),
        ],
        out_specs=pl.BlockSpec((tm, tn), lambda i, j: (i, j)),
        out_shape=jax.ShapeDtypeStruct((m, D_PROJ), BF16),
        scratch_shapes=[
            pltpu.VMEM((tm, D_MODEL), BF16),
            pltpu.VMEM((POOL_HALO, D_POOL), F32),
            pltpu.VMEM((tm + POOL_HALO, tn), F32),
        ],
        compiler_params=pltpu.CompilerParams(
            dimension_semantics=("arbitrary", "arbitrary"),
            vmem_limit_bytes=VMEM_LIMIT_BYTES),
        name="in_projection",
    )(x2d, g1, w_in, cos_t, sin_t, wpg, pscale)


def _retention_kernel(q_ref, k_ref, v_ref, rg_ref, dmask_ref, qdec_ref, kdec_ref, bdec_ref,
                      o_ref, state_ref):
    t = pl.program_id(1)

    @pl.when(t == 0)
    def _():
        state_ref[...] = jnp.zeros_like(state_ref)

    for h in range(N_HEADS):
        cs = slice(h * HEAD_DIM, (h + 1) * HEAD_DIM)
        q = q_ref[:, cs]
        k = k_ref[:, cs]
        v = v_ref[:, cs]
        scores = lax.dot_general(q, k, (((1,), (1,)), ((), ())), preferred_element_type=F32)
        p = (scores * dmask_ref[h]).astype(BF16)
        o = jnp.dot(p, v, preferred_element_type=F32)
        st = state_ref[h]
        o = o + jnp.dot(q, st.astype(BF16), preferred_element_type=F32) * qdec_ref[h]
        kd = (k.astype(F32) * kdec_ref[h]).astype(BF16)
        upd = lax.dot_general(kd, v, (((0,), (0,)), ((), ())), preferred_element_type=F32)
        state_ref[h] = st * bdec_ref[h] + upd
        o_ref[:, cs] = (_rms_scale(o) * rg_ref[:, cs].astype(F32)).astype(BF16)


def _retention(proj, tables, *, batch, seq):
    dmask, qdec, kdec, bdec = tables
    blk = RET_BLOCK
    nblk = seq // blk
    hb = D_RET // D_RET

    def seg(c):
        return pl.BlockSpec((blk, D_RET), lambda b, t, c=c: (b * nblk + t, c))

    tab = pl.BlockSpec((N_HEADS, blk, HEAD_DIM), lambda b, t: (0, 0, 0))
    del hb
    return pl.pallas_call(
        _retention_kernel,
        grid=(batch, nblk),
        in_specs=[seg(0), seg(1), seg(2), seg(3), tab, tab, tab,
                  pl.BlockSpec((N_HEADS, 1, HEAD_DIM), lambda b, t: (0, 0, 0))],
        out_specs=pl.BlockSpec((blk, D_RET), lambda b, t: (b * nblk + t, 0)),
        out_shape=jax.ShapeDtypeStruct((batch * seq, D_RET), BF16),
        scratch_shapes=[pltpu.VMEM((N_HEADS, HEAD_DIM, HEAD_DIM), F32)],
        compiler_params=pltpu.CompilerParams(
            dimension_semantics=("arbitrary", "arbitrary"),
            vmem_limit_bytes=VMEM_LIMIT_BYTES),
        name="retention",
    )(proj, proj, proj, proj, dmask, qdec, kdec, bdec)


def _retention_tables():
    blk = RET_BLOCK
    log_g = jnp.log(1.0 - 2.0 ** (-5.0 - jnp.arange(N_HEADS, dtype=F32)))
    n = jnp.arange(blk, dtype=F32)
    dist = jnp.abs(n[:, None] - n[None, :])
    chunk = jnp.arange(blk) // CHUNK
    visible = (chunk[None, :] <= chunk[:, None]).astype(F32)
    dmask = jnp.exp(log_g[:, None, None] * dist[None]) * visible[None]
    qdec = jnp.exp(log_g[:, None] * (n[None, :] + 1.0))
    kdec = jnp.exp(log_g[:, None] * (blk - 1.0 - n[None, :]))
    bdec = jnp.exp(log_g * blk)
    qdec = jnp.broadcast_to(qdec[:, :, None], (N_HEADS, blk, HEAD_DIM))
    kdec = jnp.broadcast_to(kdec[:, :, None], (N_HEADS, blk, HEAD_DIM))
    bdec = jnp.broadcast_to(bdec[:, None, None], (N_HEADS, 1, HEAD_DIM))
    return dmask, qdec, kdec, bdec


def _merge_kernel(og_ref, pool_ref, gr_ref, gp_ref, wr_ref, wp_ref, o_ref):
    y_ret = jnp.dot(og_ref[...], wr_ref[...], preferred_element_type=F32)
    y_pool = jnp.dot(pool_ref[...], wp_ref[...], preferred_element_type=F32)
    merged = gr_ref[...].astype(F32) * y_ret + gp_ref[...].astype(F32) * y_pool
    o_ref[...] = merged.astype(BF16)


def _merge(og, proj, w_ret, w_pool, *, tm=1024, tn=512):
    m = og.shape[0]
    pool_blk = (4 * D_RET) // D_POOL
    gate_ret_blk = (4 * D_RET + D_POOL) // tn
    gate_pool_blk = gate_ret_blk + D_MODEL // tn
    return pl.pallas_call(
        _merge_kernel,
        grid=(m // tm, D_MODEL // tn),
        in_specs=[
            pl.BlockSpec((tm, D_RET), lambda i, j: (i, 0)),
            pl.BlockSpec((tm, D_POOL), lambda i, j: (i, pool_blk)),
            pl.BlockSpec((tm, tn), lambda i, j: (i, gate_ret_blk + j)),
            pl.BlockSpec((tm, tn), lambda i, j: (i, gate_pool_blk + j)),
            pl.BlockSpec((D_RET, tn), lambda i, j: (0, j)),
            pl.BlockSpec((D_POOL, tn), lambda i, j: (0, j)),
        ],
        out_specs=pl.BlockSpec((tm, tn), lambda i, j: (i, j)),
        out_shape=jax.ShapeDtypeStruct((m, D_MODEL), BF16),
        compiler_params=pltpu.CompilerParams(
            dimension_semantics=("arbitrary", "arbitrary"),
            vmem_limit_bytes=VMEM_LIMIT_BYTES),
        name="branch_merge",
    )(og, proj, proj, proj, w_ret, w_pool)


def _outproj_kernel(m_ref, w_ref, x_ref, o_ref):
    o_ref[...] = x_ref[...] + jnp.dot(m_ref[...], w_ref[...], preferred_element_type=F32)


def _out_projection(merged, w_out, x2d, *, tm=1024, tn=512):
    m = merged.shape[0]
    return pl.pallas_call(
        _outproj_kernel,
        grid=(m // tm, D_MODEL // tn),
        in_specs=[
            pl.BlockSpec((tm, D_MODEL), lambda i, j: (i, 0)),
            pl.BlockSpec((D_MODEL, tn), lambda i, j: (0, j)),
            pl.BlockSpec((tm, tn), lambda i, j: (i, j)),
        ],
        out_specs=pl.BlockSpec((tm, tn), lambda i, j: (i, j)),
        out_shape=jax.ShapeDtypeStruct((m, D_MODEL), F32),
        compiler_params=pltpu.CompilerParams(
            dimension_semantics=("arbitrary", "arbitrary"),
            vmem_limit_bytes=VMEM_LIMIT_BYTES),
        name="out_projection",
    )(merged, w_out, x2d)


def _ffn_kernel(h_ref, g2_ref, wa_ref, wb_ref, wo_ref, gf_ref, o_ref, u_ref, acc_ref, *, n_ff):
    j = pl.program_id(1)

    @pl.when(j == 0)
    def _():
        _norm_rows_to(u_ref, h_ref, g2_ref)

    u = u_ref[...]
    a = jnp.dot(u, wa_ref[...], preferred_element_type=F32)
    b = jnp.dot(u, wb_ref[...], preferred_element_type=F32)
    gated = (a * _sigmoid(a) * b).astype(BF16)
    contrib = jnp.dot(gated, wo_ref[...], preferred_element_type=F32)

    @pl.when(j == 0)
    def _():
        acc_ref[...] = contrib

    @pl.when(j > 0)
    def _():
        acc_ref[...] += contrib

    @pl.when(j == n_ff - 1)
    def _():
        h2 = h_ref[...] + acc_ref[...]
        o_ref[...] = _rms_scale(h2) * gf_ref[...]


def _ffn(h1, g2, w_ffn_in, w_ffn_out, g_final, *, tm=512, tf=512):
    m = h1.shape[0]
    n_ff = D_FF // tf
    kern = functools.partial(_ffn_kernel, n_ff=n_ff)
    return pl.pallas_call(
        kern,
        grid=(m // tm, n_ff),
        in_specs=[
            pl.BlockSpec((tm, D_MODEL), lambda i, j: (i, 0)),
            pl.BlockSpec((1, D_MODEL), lambda i, j: (0, 0)),
            pl.BlockSpec((D_MODEL, tf), lambda i, j: (0, j)),
            pl.BlockSpec((D_MODEL, tf), lambda i, j: (0, n_ff + j)),
            pl.BlockSpec((tf, D_MODEL), lambda i, j: (j, 0)),
            pl.BlockSpec((1, D_MODEL), lambda i, j: (0, 0)),
        ],
        out_specs=pl.BlockSpec((tm, D_MODEL), lambda i, j: (i, 0)),
        out_shape=jax.ShapeDtypeStruct((m, D_MODEL), F32),
        scratch_shapes=[pltpu.VMEM((tm, D_MODEL), BF16), pltpu.VMEM((tm, D_MODEL), F32)],
        compiler_params=pltpu.CompilerParams(
            dimension_semantics=("arbitrary", "arbitrary"),
            vmem_limit_bytes=VMEM_LIMIT_BYTES),
        name="swiglu_ffn",
    )(h1, g2, w_ffn_in, w_ffn_in, w_ffn_out, g_final)


def _rotary_tables(seq):
    half = HEAD_DIM // 2
    inv_freq = 1.0 / (ROPE_BASE ** (jnp.arange(0, HEAD_DIM, 2, dtype=F32) / HEAD_DIM))
    ang = jnp.arange(seq).astype(F32)[:, None] * inv_freq[None, :]
    scale = HEAD_DIM ** -0.5
    del half
    return jnp.cos(ang) * scale, jnp.sin(ang) * scale


def kernel(x, norm1_g, w_in, w_ret_branch, w_pool_group, pool_scale, w_pool_branch, w_out,
           norm2_g, w_ffn_in, w_ffn_out, norm_final_g):
    batch, seq, d = x.shape
    assert d == D_MODEL and norm1_g.shape[0] == 1 and seq % 1024 == 0
    assert math.log2(HEAD_DIM ** -0.5) == -4.0
    x2d = x.reshape(batch * seq, D_MODEL)
    cos_t, sin_t = _rotary_tables(seq)

    proj = _in_projection(
        x2d, norm1_g[0][None, :], w_in[0].astype(BF16), cos_t, sin_t,
        w_pool_group[0].astype(BF16), pool_scale[0][None, :], seq=seq)
    og = _retention(proj, _retention_tables(), batch=batch, seq=seq)
    merged = _merge(og, proj, w_ret_branch[0].astype(BF16), w_pool_branch[0].astype(BF16))
    h1 = _out_projection(merged, w_out[0].astype(BF16), x2d)
    out = _ffn(h1, norm2_g[0][None, :], w_ffn_in[0].astype(BF16), w_ffn_out[0].astype(BF16),
               norm_final_g[None, :])
    return out.reshape(batch, seq, D_MODEL)
```

```python
import functools
import math

import jax
import jax.numpy as jnp
from jax import lax
from jax.experimental import pallas as pl
from jax.experimental.pallas import tpu as pltpu

D_MODEL = 2048
CHUNK = 64
N_HEADS = 8
HEAD_DIM = 256
D_RET = N_HEADS * HEAD_DIM
POOL_WINDOWS = (2, 4, 8, 16)
POOL_GROUP_DIM = 256
D_POOL = 1024
D_FF = 5632
D_PROJ = 4 * D_RET + D_POOL + 2 * D_MODEL
ROPE_BASE = 10000.0
NORM_EPS = 1e-6
POOL_HALO = 16

VMEM_LIMIT_BYTES = 56 * 1024 * 1024
RET_BLOCK = 256
PROJ_TM = 1024
PROJ_TN = 1024

F32 = jnp.float32
BF16 = jnp.bfloat16


def _rms_scale(xf):
    return xf * lax.rsqrt(jnp.mean(xf * xf, axis=-1, keepdims=True) + NORM_EPS)


def _sigmoid(x):
    return 1.0 / (1.0 + jnp.exp(-x))


def _norm_rows_to(dst_refs, src_ref, gain_ref, rows_per_chunk=256):
    n_chunks = src_ref.shape[0] // rows_per_chunk

    def body(c, carry):
        r0 = pl.multiple_of(c * rows_per_chunk, rows_per_chunk)
        xf = src_ref[pl.ds(r0, rows_per_chunk), :]
        u = (_rms_scale(xf) * gain_ref[...]).astype(BF16)
        for dst_ref in dst_refs:
            dst_ref[pl.ds(r0, rows_per_chunk), :] = u
        return carry

    lax.fori_loop(0, n_chunks, body, 0)


def _params():
    return pltpu.CompilerParams(
        dimension_semantics=("arbitrary", "arbitrary"),
        vmem_limit_bytes=VMEM_LIMIT_BYTES)


def _qk_kernel(x_ref, g_ref, w_ref, cos_ref, sin_ref, o_ref, uo_ref, u_ref):
    @pl.when(pl.program_id(1) == 0)
    def _():
        _norm_rows_to((u_ref, uo_ref), x_ref, g_ref)

    acc = jnp.dot(u_ref[...], w_ref[...], preferred_element_type=F32)
    cos = cos_ref[...]
    sin = sin_ref[...]
    half = HEAD_DIM // 2
    for hh in range(acc.shape[1] // HEAD_DIM):
        c0 = hh * HEAD_DIM
        t1 = acc[:, c0:c0 + half]
        t2 = acc[:, c0 + half:c0 + HEAD_DIM]
        o_ref[:, c0:c0 + half] = (t1 * cos - t2 * sin).astype(BF16)
        o_ref[:, c0 + half:c0 + HEAD_DIM] = (t1 * sin + t2 * cos).astype(BF16)


def _qk_projection(x2d, g1, w_in, cos_t, sin_t, *, seq):
    m = x2d.shape[0]
    tm, tn = PROJ_TM, PROJ_TN
    bps = seq // tm
    return pl.pallas_call(
        _qk_kernel,
        grid=(m // tm, 2 * D_RET // tn),
        in_specs=[
            pl.BlockSpec((tm, D_MODEL), lambda i, j: (i, 0)),
            pl.BlockSpec((1, D_MODEL), lambda i, j: (0, 0)),
            pl.BlockSpec((D_MODEL, tn), lambda i, j: (0, j)),
            pl.BlockSpec((tm, HEAD_DIM // 2), lambda i, j: (i % bps, 0)),
            pl.BlockSpec((tm, HEAD_DIM // 2), lambda i, j: (i % bps, 0)),
        ],
        out_specs=[
            pl.BlockSpec((tm, tn), lambda i, j: (i, j)),
            pl.BlockSpec((tm, D_MODEL), lambda i, j: (i, 0)),
        ],
        out_shape=[
            jax.ShapeDtypeStruct((m, 2 * D_RET), BF16),
            jax.ShapeDtypeStruct((m, D_MODEL), BF16),
        ],
        scratch_shapes=[pltpu.VMEM((tm, D_MODEL), BF16)],
        compiler_params=_params(),
        name="qk_projection",
    )(x2d, g1, w_in, cos_t, sin_t)


def _vg_kernel(u_ref, w_ref, o_ref, *, n_v, n_rg):
    j = pl.program_id(1)
    acc = jnp.dot(u_ref[...], w_ref[...], preferred_element_type=F32)
    sig = _sigmoid(acc)
    is_v = j < n_v
    is_gate = j >= n_v + n_rg
    out = jnp.where(is_v, acc, sig * jnp.where(is_gate, jnp.ones_like(acc), acc))
    o_ref[...] = out.astype(BF16)


def _vg_projection(u, w_in):
    m = u.shape[0]
    tm, tn = PROJ_TM, PROJ_TN
    n_v = D_RET // tn
    n_rg = D_RET // tn
    n_gate = 2 * D_MODEL // tn
    w_v0 = 2 * D_RET // tn
    w_gate_shift = D_POOL // tn
    kern = functools.partial(_vg_kernel, n_v=n_v, n_rg=n_rg)

    def w_map(i, j):
        return (0, w_v0 + j + jnp.where(j >= n_v + n_rg, w_gate_shift, 0))

    return pl.pallas_call(
        kern,
        grid=(m // tm, n_v + n_rg + n_gate),
        in_specs=[
            pl.BlockSpec((tm, D_MODEL), lambda i, j: (i, 0)),
            pl.BlockSpec((D_MODEL, tn), w_map),
        ],
        out_specs=pl.BlockSpec((tm, tn), lambda i, j: (i, j)),
        out_shape=jax.ShapeDtypeStruct((m, 2 * D_RET + 2 * D_MODEL), BF16),
        compiler_params=_params(),
        name="vg_projection",
    )(u, w_in)


def _pool_kernel(u_ref, w_ref, wpg_ref, pscale_ref, o_ref, halo_ref, ext_ref, *, bps):
    tm = u_ref.shape[0]
    seq_block = pl.program_id(0) % bps
    acc = jnp.dot(u_ref[...], w_ref[...], preferred_element_type=F32)
    prev = halo_ref[...]
    ext_ref[0:POOL_HALO, :] = jnp.where(seq_block == 0, jnp.zeros_like(prev), prev)
    ext_ref[POOL_HALO:POOL_HALO + tm, :] = acc
    halo_ref[...] = acc[tm - POOL_HALO:, :]
    pos = seq_block * tm + lax.broadcasted_iota(jnp.int32, (tm, 1), 0)
    for grp, win in enumerate(POOL_WINDOWS):
        cs = slice(grp * POOL_GROUP_DIM, (grp + 1) * POOL_GROUP_DIM)
        s = ext_ref[POOL_HALO:POOL_HALO + tm, cs]
        for d in range(1, win):
            s = s + ext_ref[POOL_HALO - d:POOL_HALO - d + tm, cs]
        inv_cnt = 1.0 / jnp.minimum(pos + 1, win).astype(F32)
        pooled = s * inv_cnt - acc[:, cs]
        y = jnp.dot(pooled.astype(BF16), wpg_ref[grp], preferred_element_type=F32)
        o_ref[:, cs] = (y * pscale_ref[:, cs]).astype(BF16)


def _pool_projection(u, w_in, wpg, pscale, *, seq):
    m = u.shape[0]
    tm = PROJ_TM
    kern = functools.partial(_pool_kernel, bps=seq // tm)
    return pl.pallas_call(
        kern,
        grid=(m // tm,),
        in_specs=[
            pl.BlockSpec((tm, D_MODEL), lambda i: (i, 0)),
            pl.BlockSpec((D_MODEL, D_POOL), lambda i: (0, 4 * D_RET // D_POOL)),
            pl.BlockSpec((len(POOL_WINDOWS), POOL_GROUP_DIM, POOL_GROUP_DIM), lambda i: (0, 0, 0)),
            pl.BlockSpec((1, D_POOL), lambda i: (0, 0)),
        ],
        out_specs=pl.BlockSpec((tm, D_POOL), lambda i: (i, 0)),
        out_shape=jax.ShapeDtypeStruct((m, D_POOL), BF16),
        scratch_shapes=[
            pltpu.VMEM((POOL_HALO, D_POOL), F32),
            pltpu.VMEM((tm + POOL_HALO, D_POOL), F32),
        ],
        compiler_params=pltpu.CompilerParams(
            dimension_semantics=("arbitrary",), vmem_limit_bytes=VMEM_LIMIT_BYTES),
        name="pool_projection",
    )(u, w_in, wpg, pscale)


def _retention_kernel(q_ref, k_ref, v_ref, rg_ref, dmask_ref, qdec_ref, kdec_ref, bdec_ref,
                      o_ref, state_ref):
    @pl.when(pl.program_id(1) == 0)
    def _():
        state_ref[...] = jnp.zeros_like(state_ref)

    for h in range(N_HEADS):
        cs = slice(h * HEAD_DIM, (h + 1) * HEAD_DIM)
        q = q_ref[:, cs]
        k = k_ref[:, cs]
        v = v_ref[:, cs]
        scores = lax.dot_general(q, k, (((1,), (1,)), ((), ())), preferred_element_type=F32)
        p = (scores * dmask_ref[h]).astype(BF16)
        o = jnp.dot(p, v, preferred_element_type=F32)
        st = state_ref[h]
        o = o + jnp.dot(q, st.astype(BF16), preferred_element_type=F32) * qdec_ref[h]
        kd = (k.astype(F32) * kdec_ref[h]).astype(BF16)
        upd = lax.dot_general(kd, v, (((0,), (0,)), ((), ())), preferred_element_type=F32)
        state_ref[h] = st * bdec_ref[h] + upd
        o_ref[:, cs] = (_rms_scale(o) * rg_ref[:, cs].astype(F32)).astype(BF16)


def _retention(qk, vg, tables, *, batch, seq):
    dmask, qdec, kdec, bdec = tables
    blk = RET_BLOCK
    nblk = seq // blk

    def seg(c):
        return pl.BlockSpec((blk, D_RET), lambda b, t: (b * nblk + t, c))

    tab = pl.BlockSpec((N_HEADS, blk, HEAD_DIM), lambda b, t: (0, 0, 0))
    return pl.pallas_call(
        _retention_kernel,
        grid=(batch, nblk),
        in_specs=[seg(0), seg(1), seg(0), seg(1), tab, tab, tab,
                  pl.BlockSpec((N_HEADS, 1, HEAD_DIM), lambda b, t: (0, 0, 0))],
        out_specs=pl.BlockSpec((blk, D_RET), lambda b, t: (b * nblk + t, 0)),
        out_shape=jax.ShapeDtypeStruct((batch * seq, D_RET), BF16),
        scratch_shapes=[pltpu.VMEM((N_HEADS, HEAD_DIM, HEAD_DIM), F32)],
        compiler_params=_params(),
        name="retention",
    )(qk, qk, vg, vg, dmask, qdec, kdec, bdec)


def _retention_tables():
    blk = RET_BLOCK
    log_g = jnp.log(1.0 - 2.0 ** (-5.0 - jnp.arange(N_HEADS, dtype=F32)))
    n = jnp.arange(blk, dtype=F32)
    dist = jnp.abs(n[:, None] - n[None, :])
    chunk = jnp.arange(blk) // CHUNK
    visible = (chunk[None, :] <= chunk[:, None]).astype(F32)
    dmask = jnp.exp(log_g[:, None, None] * dist[None]) * visible[None]
    qdec = jnp.exp(log_g[:, None] * (n[None, :] + 1.0))
    kdec = jnp.exp(log_g[:, None] * (blk - 1.0 - n[None, :]))
    bdec = jnp.exp(log_g * blk)
    qdec = jnp.broadcast_to(qdec[:, :, None], (N_HEADS, blk, HEAD_DIM))
    kdec = jnp.broadcast_to(kdec[:, :, None], (N_HEADS, blk, HEAD_DIM))
    bdec = jnp.broadcast_to(bdec[:, None, None], (N_HEADS, 1, HEAD_DIM))
    return dmask, qdec, kdec, bdec


def _merge_kernel(og_ref, pool_ref, gr_ref, gp_ref, wr_ref, wp_ref, o_ref):
    y_ret = jnp.dot(og_ref[...], wr_ref[...], preferred_element_type=F32)
    y_pool = jnp.dot(pool_ref[...], wp_ref[...], preferred_element_type=F32)
    merged = gr_ref[...].astype(F32) * y_ret + gp_ref[...].astype(F32) * y_pool
    o_ref[...] = merged.astype(BF16)


def _merge(og, pooled, vg, w_ret, w_pool, *, tm=1024, tn=512):
    m = og.shape[0]
    gate_ret_blk = 2 * D_RET // tn
    gate_pool_blk = gate_ret_blk + D_MODEL // tn
    return pl.pallas_call(
        _merge_kernel,
        grid=(m // tm, D_MODEL // tn),
        in_specs=[
            pl.BlockSpec((tm, D_RET), lambda i, j: (i, 0)),
            pl.BlockSpec((tm, D_POOL), lambda i, j: (i, 0)),
            pl.BlockSpec((tm, tn), lambda i, j: (i, gate_ret_blk + j)),
            pl.BlockSpec((tm, tn), lambda i, j: (i, gate_pool_blk + j)),
            pl.BlockSpec((D_RET, tn), lambda i, j: (0, j)),
            pl.BlockSpec((D_POOL, tn), lambda i, j: (0, j)),
        ],
        out_specs=pl.BlockSpec((tm, tn), lambda i, j: (i, j)),
        out_shape=jax.ShapeDtypeStruct((m, D_MODEL), BF16),
        compiler_params=_params(),
        name="branch_merge",
    )(og, pooled, vg, vg, w_ret, w_pool)


def _outproj_kernel(m_ref, w_ref, x_ref, o_ref):
    o_ref[...] = x_ref[...] + jnp.dot(m_ref[...], w_ref[...], preferred_element_type=F32)


def _out_projection(merged, w_out, x2d, *, tm=1024, tn=512):
    m = merged.shape[0]
    return pl.pallas_call(
        _outproj_kernel,
        grid=(m // tm, D_MODEL // tn),
        in_specs=[
            pl.BlockSpec((tm, D_MODEL), lambda i, j: (i, 0)),
            pl.BlockSpec((D_MODEL, tn), lambda i, j: (0, j)),
            pl.BlockSpec((tm, tn), lambda i, j: (i, j)),
        ],
        out_specs=pl.BlockSpec((tm, tn), lambda i, j: (i, j)),
        out_shape=jax.ShapeDtypeStruct((m, D_MODEL), F32),
        compiler_params=_params(),
        name="out_projection",
    )(merged, w_out, x2d)


def _ffn_kernel(h_ref, g2_ref, wa_ref, wb_ref, wo_ref, gf_ref, o_ref, u_ref, *, n_ff):
    j = pl.program_id(1)

    @pl.when(j == 0)
    def _():
        _norm_rows_to((u_ref,), h_ref, g2_ref)
        o_ref[...] = jnp.zeros_like(o_ref)

    u = u_ref[...]
    a = jnp.dot(u, wa_ref[...], preferred_element_type=F32)
    b = jnp.dot(u, wb_ref[...], preferred_element_type=F32)
    gated = (a * _sigmoid(a) * b).astype(BF16)
    o_ref[...] += jnp.dot(gated, wo_ref[...], preferred_element_type=F32)

    @pl.when(j == n_ff - 1)
    def _():
        h2 = h_ref[...] + o_ref[...]
        o_ref[...] = _rms_scale(h2) * gf_ref[...]


def _ffn(h1, g2, w_ffn_in, w_ffn_out, g_final, *, tm=512, tf=512):
    m = h1.shape[0]
    n_ff = D_FF // tf
    kern = functools.partial(_ffn_kernel, n_ff=n_ff)
    return pl.pallas_call(
        kern,
        grid=(m // tm, n_ff),
        in_specs=[
            pl.BlockSpec((tm, D_MODEL), lambda i, j: (i, 0)),
            pl.BlockSpec((1, D_MODEL), lambda i, j: (0, 0)),
            pl.BlockSpec((D_MODEL, tf), lambda i, j: (0, j)),
            pl.BlockSpec((D_MODEL, tf), lambda i, j: (0, n_ff + j)),
            pl.BlockSpec((tf, D_MODEL), lambda i, j: (j, 0)),
            pl.BlockSpec((1, D_MODEL), lambda i, j: (0, 0)),
        ],
        out_specs=pl.BlockSpec((tm, D_MODEL), lambda i, j: (i, 0)),
        out_shape=jax.ShapeDtypeStruct((m, D_MODEL), F32),
        scratch_shapes=[pltpu.VMEM((tm, D_MODEL), BF16)],
        compiler_params=_params(),
        name="swiglu_ffn",
    )(h1, g2, w_ffn_in, w_ffn_in, w_ffn_out, g_final)


def _rotary_tables(seq):
    inv_freq = 1.0 / (ROPE_BASE ** (jnp.arange(0, HEAD_DIM, 2, dtype=F32) / HEAD_DIM))
    ang = jnp.arange(seq).astype(F32)[:, None] * inv_freq[None, :]
    scale = HEAD_DIM ** -0.5
    return jnp.cos(ang) * scale, jnp.sin(ang) * scale


def kernel(x, norm1_g, w_in, w_ret_branch, w_pool_group, pool_scale, w_pool_branch, w_out,
           norm2_g, w_ffn_in, w_ffn_out, norm_final_g):
    batch, seq, d = x.shape
    assert d == D_MODEL and norm1_g.shape[0] == 1 and seq % PROJ_TM == 0
    assert math.log2(HEAD_DIM ** -0.5) == -4.0
    x2d = x.reshape(batch * seq, D_MODEL)
    cos_t, sin_t = _rotary_tables(seq)
    w_in_b = w_in[0].astype(BF16)

    qk, u = _qk_projection(x2d, norm1_g[0][None, :], w_in_b, cos_t, sin_t, seq=seq)
    vg = _vg_projection(u, w_in_b)
    pooled = _pool_projection(u, w_in_b, w_pool_group[0].astype(BF16), pool_scale[0][None, :],
                              seq=seq)
    og = _retention(qk, vg, _retention_tables(), batch=batch, seq=seq)
    merged = _merge(og, pooled, vg, w_ret_branch[0].astype(BF16), w_pool_branch[0].astype(BF16))
    h1 = _out_projection(merged, w_out[0].astype(BF16), x2d)
    out = _ffn(h1, norm2_g[0][None, :], w_ffn_in[0].astype(BF16), w_ffn_out[0].astype(BF16),
               norm_final_g[None, :])
    return out.reshape(batch, seq, D_MODEL)
```

```python
import functools
import math

import jax
import jax.numpy as jnp
from jax import lax
from jax.experimental import pallas as pl
from jax.experimental.pallas import tpu as pltpu

D_MODEL = 2048
CHUNK = 64
N_HEADS = 8
HEAD_DIM = 256
D_RET = N_HEADS * HEAD_DIM
POOL_WINDOWS = (2, 4, 8, 16)
POOL_GROUP_DIM = 256
D_POOL = 1024
D_FF = 5632
D_PROJ = 4 * D_RET + D_POOL + 2 * D_MODEL
ROPE_BASE = 10000.0
NORM_EPS = 1e-6
POOL_HALO = 16

VMEM_LIMIT_BYTES = 56 * 1024 * 1024
RET_BLOCK = 256
PROJ_TM = 1024
PROJ_TN = 1024
PROJ_PIECE = 256
FFN_TM = 1024
FFN_TF = 256
FFN_OUT_PIECE = 512

F32 = jnp.float32
BF16 = jnp.bfloat16


def _rms_scale(xf):
    return xf * lax.rsqrt(jnp.mean(xf * xf, axis=-1, keepdims=True) + NORM_EPS)


def _sigmoid(x):
    return 0.5 * jnp.tanh(0.5 * x) + 0.5


def _norm_rows_to(dst_refs, src_ref, gain_ref, rows_per_chunk=256):
    n_chunks = src_ref.shape[0] // rows_per_chunk

    def body(c, carry):
        r0 = pl.multiple_of(c * rows_per_chunk, rows_per_chunk)
        xf = src_ref[pl.ds(r0, rows_per_chunk), :]
        u = (_rms_scale(xf) * gain_ref[...]).astype(BF16)
        for dst_ref in dst_refs:
            dst_ref[pl.ds(r0, rows_per_chunk), :] = u
        return carry

    lax.fori_loop(0, n_chunks, body, 0)


def _params():
    return pltpu.CompilerParams(
        dimension_semantics=("arbitrary", "arbitrary"),
        vmem_limit_bytes=VMEM_LIMIT_BYTES)


def _qk_kernel(x_ref, g_ref, w_ref, cos_ref, sin_ref, o_ref, uo_ref, u_ref):
    @pl.when(pl.program_id(1) == 0)
    def _():
        _norm_rows_to((u_ref, uo_ref), x_ref, g_ref)

    acc = jnp.dot(u_ref[...], w_ref[...], preferred_element_type=F32)
    cos = cos_ref[...]
    sin = sin_ref[...]
    half = HEAD_DIM // 2
    for hh in range(acc.shape[1] // HEAD_DIM):
        c0 = hh * HEAD_DIM
        t1 = acc[:, c0:c0 + half]
        t2 = acc[:, c0 + half:c0 + HEAD_DIM]
        o_ref[:, c0:c0 + half] = (t1 * cos - t2 * sin).astype(BF16)
        o_ref[:, c0 + half:c0 + HEAD_DIM] = (t1 * sin + t2 * cos).astype(BF16)


def _qk_projection(x2d, g1, w_in, cos_t, sin_t, *, seq):
    m = x2d.shape[0]
    tm, tn = PROJ_TM, PROJ_TN
    bps = seq // tm
    return pl.pallas_call(
        _qk_kernel,
        grid=(m // tm, 2 * D_RET // tn),
        in_specs=[
            pl.BlockSpec((tm, D_MODEL), lambda i, j: (i, 0)),
            pl.BlockSpec((1, D_MODEL), lambda i, j: (0, 0)),
            pl.BlockSpec((D_MODEL, tn), lambda i, j: (0, j)),
            pl.BlockSpec((tm, HEAD_DIM // 2), lambda i, j: (i % bps, 0)),
            pl.BlockSpec((tm, HEAD_DIM // 2), lambda i, j: (i % bps, 0)),
        ],
        out_specs=[
            pl.BlockSpec((tm, tn), lambda i, j: (i, j)),
            pl.BlockSpec((tm, D_MODEL), lambda i, j: (i, 0)),
        ],
        out_shape=[
            jax.ShapeDtypeStruct((m, 2 * D_RET), BF16),
            jax.ShapeDtypeStruct((m, D_MODEL), BF16),
        ],
        scratch_shapes=[pltpu.VMEM((tm, D_MODEL), BF16)],
        compiler_params=_params(),
        name="qk_projection",
    )(x2d, g1, w_in, cos_t, sin_t)


def _vg_kernel(u_ref, w_ref, o_ref, *, n_v, n_rg):
    j = pl.program_id(1)

    def project(epilogue):
        for c0 in range(0, o_ref.shape[1], PROJ_PIECE):
            cs = slice(c0, c0 + PROJ_PIECE)
            acc = jnp.dot(u_ref[...], w_ref[:, cs], preferred_element_type=F32)
            o_ref[:, cs] = epilogue(acc).astype(BF16)

    @pl.when(j < n_v)
    def _():
        project(lambda acc: acc)

    @pl.when(jnp.logical_and(j >= n_v, j < n_v + n_rg))
    def _():
        project(lambda acc: acc * _sigmoid(acc))

    @pl.when(j >= n_v + n_rg)
    def _():
        project(_sigmoid)


def _vg_projection(u, w_in):
    m = u.shape[0]
    tm, tn = PROJ_TM, PROJ_TN
    n_v = D_RET // tn
    n_rg = D_RET // tn
    n_gate = 2 * D_MODEL // tn
    w_v0 = 2 * D_RET // tn
    w_gate_shift = D_POOL // tn
    kern = functools.partial(_vg_kernel, n_v=n_v, n_rg=n_rg)

    def w_map(i, j):
        return (0, w_v0 + j + jnp.where(j >= n_v + n_rg, w_gate_shift, 0))

    return pl.pallas_call(
        kern,
        grid=(m // tm, n_v + n_rg + n_gate),
        in_specs=[
            pl.BlockSpec((tm, D_MODEL), lambda i, j: (i, 0)),
            pl.BlockSpec((D_MODEL, tn), w_map),
        ],
        out_specs=pl.BlockSpec((tm, tn), lambda i, j: (i, j)),
        out_shape=jax.ShapeDtypeStruct((m, 2 * D_RET + 2 * D_MODEL), BF16),
        compiler_params=_params(),
        name="vg_projection",
    )(u, w_in)


def _pool_kernel(u_ref, w_ref, wpg_ref, pscale_ref, o_ref, halo_ref, ext_ref, *, bps):
    tm = u_ref.shape[0]
    seq_block = pl.program_id(0) % bps
    acc = jnp.dot(u_ref[...], w_ref[...], preferred_element_type=F32)
    prev = halo_ref[...]
    ext_ref[0:POOL_HALO, :] = jnp.where(seq_block == 0, jnp.zeros_like(prev), prev)
    ext_ref[POOL_HALO:POOL_HALO + tm, :] = acc
    halo_ref[...] = acc[tm - POOL_HALO:, :]
    pos = seq_block * tm + lax.broadcasted_iota(jnp.int32, (tm, 1), 0)
    for grp, win in enumerate(POOL_WINDOWS):
        cs = slice(grp * POOL_GROUP_DIM, (grp + 1) * POOL_GROUP_DIM)
        s = ext_ref[POOL_HALO:POOL_HALO + tm, cs]
        for d in range(1, win):
            s = s + ext_ref[POOL_HALO - d:POOL_HALO - d + tm, cs]
        inv_cnt = 1.0 / jnp.minimum(pos + 1, win).astype(F32)
        pooled = s * inv_cnt - acc[:, cs]
        y = jnp.dot(pooled.astype(BF16), wpg_ref[grp], preferred_element_type=F32)
        o_ref[:, cs] = (y * pscale_ref[:, cs]).astype(BF16)


def _pool_projection(u, w_in, wpg, pscale, *, seq):
    m = u.shape[0]
    tm = PROJ_TM
    kern = functools.partial(_pool_kernel, bps=seq // tm)
    return pl.pallas_call(
        kern,
        grid=(m // tm,),
        in_specs=[
            pl.BlockSpec((tm, D_MODEL), lambda i: (i, 0)),
            pl.BlockSpec((D_MODEL, D_POOL), lambda i: (0, 4 * D_RET // D_POOL)),
            pl.BlockSpec((len(POOL_WINDOWS), POOL_GROUP_DIM, POOL_GROUP_DIM), lambda i: (0, 0, 0)),
            pl.BlockSpec((1, D_POOL), lambda i: (0, 0)),
        ],
        out_specs=pl.BlockSpec((tm, D_POOL), lambda i: (i, 0)),
        out_shape=jax.ShapeDtypeStruct((m, D_POOL), BF16),
        scratch_shapes=[
            pltpu.VMEM((POOL_HALO, D_POOL), F32),
            pltpu.VMEM((tm + POOL_HALO, D_POOL), F32),
        ],
        compiler_params=pltpu.CompilerParams(
            dimension_semantics=("arbitrary",), vmem_limit_bytes=VMEM_LIMIT_BYTES),
        name="pool_projection",
    )(u, w_in, wpg, pscale)


def _retention_kernel(q_ref, k_ref, v_ref, rg_ref, dmask_ref, qdec_ref, kdec_ref, bdec_ref,
                      o_ref, state_ref):
    @pl.when(pl.program_id(1) == 0)
    def _():
        state_ref[...] = jnp.zeros_like(state_ref)

    for h in range(N_HEADS):
        cs = slice(h * HEAD_DIM, (h + 1) * HEAD_DIM)
        q = q_ref[:, cs]
        k = k_ref[:, cs]
        v = v_ref[:, cs]
        scores = lax.dot_general(q, k, (((1,), (1,)), ((), ())), preferred_element_type=F32)
        p = (scores * dmask_ref[h]).astype(BF16)
        o = jnp.dot(p, v, preferred_element_type=F32)
        st = state_ref[h]
        o = o + jnp.dot(q, st.astype(BF16), preferred_element_type=F32) * qdec_ref[h]
        kd = (k.astype(F32) * kdec_ref[h]).astype(BF16)
        upd = lax.dot_general(kd, v, (((0,), (0,)), ((), ())), preferred_element_type=F32)
        state_ref[h] = st * bdec_ref[h] + upd
        o_ref[:, cs] = (_rms_scale(o) * rg_ref[:, cs].astype(F32)).astype(BF16)


def _retention(qk, vg, tables, *, batch, seq):
    dmask, qdec, kdec, bdec = tables
    blk = RET_BLOCK
    nblk = seq // blk

    def seg(c):
        return pl.BlockSpec((blk, D_RET), lambda b, t: (b * nblk + t, c))

    tab = pl.BlockSpec((N_HEADS, blk, HEAD_DIM), lambda b, t: (0, 0, 0))
    return pl.pallas_call(
        _retention_kernel,
        grid=(batch, nblk),
        in_specs=[seg(0), seg(1), seg(0), seg(1), tab, tab, tab,
                  pl.BlockSpec((N_HEADS, 1, HEAD_DIM), lambda b, t: (0, 0, 0))],
        out_specs=pl.BlockSpec((blk, D_RET), lambda b, t: (b * nblk + t, 0)),
        out_shape=jax.ShapeDtypeStruct((batch * seq, D_RET), BF16),
        scratch_shapes=[pltpu.VMEM((N_HEADS, HEAD_DIM, HEAD_DIM), F32)],
        compiler_params=_params(),
        name="retention",
    )(qk, qk, vg, vg, dmask, qdec, kdec, bdec)


def _retention_tables():
    blk = RET_BLOCK
    log_g = jnp.log(1.0 - 2.0 ** (-5.0 - jnp.arange(N_HEADS, dtype=F32)))
    n = jnp.arange(blk, dtype=F32)
    dist = jnp.abs(n[:, None] - n[None, :])
    chunk = jnp.arange(blk) // CHUNK
    visible = (chunk[None, :] <= chunk[:, None]).astype(F32)
    dmask = jnp.exp(log_g[:, None, None] * dist[None]) * visible[None]
    qdec = jnp.exp(log_g[:, None] * (n[None, :] + 1.0))
    kdec = jnp.exp(log_g[:, None] * (blk - 1.0 - n[None, :]))
    bdec = jnp.exp(log_g * blk)
    qdec = jnp.broadcast_to(qdec[:, :, None], (N_HEADS, blk, HEAD_DIM))
    kdec = jnp.broadcast_to(kdec[:, :, None], (N_HEADS, blk, HEAD_DIM))
    bdec = jnp.broadcast_to(bdec[:, None, None], (N_HEADS, 1, HEAD_DIM))
    return dmask, qdec, kdec, bdec


def _merge_kernel(og_ref, pool_ref, gr_ref, gp_ref, wr_ref, wp_ref, o_ref):
    y_ret = jnp.dot(og_ref[...], wr_ref[...], preferred_element_type=F32)
    y_pool = jnp.dot(pool_ref[...], wp_ref[...], preferred_element_type=F32)
    merged = gr_ref[...].astype(F32) * y_ret + gp_ref[...].astype(F32) * y_pool
    o_ref[...] = merged.astype(BF16)


def _merge(og, pooled, vg, w_ret, w_pool, *, tm=1024, tn=512):
    m = og.shape[0]
    gate_ret_blk = 2 * D_RET // tn
    gate_pool_blk = gate_ret_blk + D_MODEL // tn
    return pl.pallas_call(
        _merge_kernel,
        grid=(m // tm, D_MODEL // tn),
        in_specs=[
            pl.BlockSpec((tm, D_RET), lambda i, j: (i, 0)),
            pl.BlockSpec((tm, D_POOL), lambda i, j: (i, 0)),
            pl.BlockSpec((tm, tn), lambda i, j: (i, gate_ret_blk + j)),
            pl.BlockSpec((tm, tn), lambda i, j: (i, gate_pool_blk + j)),
            pl.BlockSpec((D_RET, tn), lambda i, j: (0, j)),
            pl.BlockSpec((D_POOL, tn), lambda i, j: (0, j)),
        ],
        out_specs=pl.BlockSpec((tm, tn), lambda i, j: (i, j)),
        out_shape=jax.ShapeDtypeStruct((m, D_MODEL), BF16),
        compiler_params=_params(),
        name="branch_merge",
    )(og, pooled, vg, vg, w_ret, w_pool)


def _outproj_kernel(m_ref, w_ref, x_ref, o_ref):
    o_ref[...] = x_ref[...] + jnp.dot(m_ref[...], w_ref[...], preferred_element_type=F32)


def _out_projection(merged, w_out, x2d, *, tm=512):
    m = merged.shape[0]
    return pl.pallas_call(
        _outproj_kernel,
        grid=(m // tm,),
        in_specs=[
            pl.BlockSpec((tm, D_MODEL), lambda i: (i, 0)),
            pl.BlockSpec((D_MODEL, D_MODEL), lambda i: (0, 0)),
            pl.BlockSpec((tm, D_MODEL), lambda i: (i, 0)),
        ],
        out_specs=pl.BlockSpec((tm, D_MODEL), lambda i: (i, 0)),
        out_shape=jax.ShapeDtypeStruct((m, D_MODEL), F32),
        compiler_params=pltpu.CompilerParams(
            dimension_semantics=("arbitrary",), vmem_limit_bytes=VMEM_LIMIT_BYTES),
        name="out_projection",
    )(merged, w_out, x2d)


def _ffn_kernel(h_ref, g2_ref, wab_ref, wo_ref, gf_ref, o_ref, u_ref, *, n_ff):
    j = pl.program_id(1)
    tf = wo_ref.shape[0]

    @pl.when(j == 0)
    def _():
        _norm_rows_to((u_ref,), h_ref, g2_ref)
        o_ref[...] = h_ref[...]

    ab = jnp.dot(u_ref[...], wab_ref[...], preferred_element_type=F32)
    a = ab[:, :tf]
    b = ab[:, tf:]
    gated = (a * _sigmoid(a) * b).astype(BF16)
    for c0 in range(0, o_ref.shape[1], FFN_OUT_PIECE):
        cs = slice(c0, c0 + FFN_OUT_PIECE)
        o_ref[:, cs] += jnp.dot(gated, wo_ref[:, cs], preferred_element_type=F32)

    @pl.when(j == n_ff - 1)
    def _():
        o_ref[...] = _rms_scale(o_ref[...]) * gf_ref[...]


def _ffn(h1, g2, w_ab, w_ffn_out, g_final, *, tm=FFN_TM, tf=FFN_TF):
    m = h1.shape[0]
    n_ff = D_FF // tf
    kern = functools.partial(_ffn_kernel, n_ff=n_ff)
    return pl.pallas_call(
        kern,
        grid=(m // tm, n_ff),
        in_specs=[
            pl.BlockSpec((tm, D_MODEL), lambda i, j: (i, 0)),
            pl.BlockSpec((1, D_MODEL), lambda i, j: (0, 0)),
            pl.BlockSpec((D_MODEL, 2 * tf), lambda i, j: (0, j)),
            pl.BlockSpec((tf, D_MODEL), lambda i, j: (j, 0)),
            pl.BlockSpec((1, D_MODEL), lambda i, j: (0, 0)),
        ],
        out_specs=pl.BlockSpec((tm, D_MODEL), lambda i, j: (i, 0)),
        out_shape=jax.ShapeDtypeStruct((m, D_MODEL), F32),
        scratch_shapes=[pltpu.VMEM((tm, D_MODEL), BF16)],
        compiler_params=_params(),
        name="swiglu_ffn",
    )(h1, g2, w_ab, w_ffn_out, g_final)


def _interleave_ffn_in(w_ffn_in, tf):
    n_ff = D_FF // tf
    w = w_ffn_in.astype(BF16).reshape(D_MODEL, 2, n_ff, tf)
    return jnp.swapaxes(w, 1, 2).reshape(D_MODEL, 2 * D_FF)


def _rotary_tables(seq):
    inv_freq = 1.0 / (ROPE_BASE ** (jnp.arange(0, HEAD_DIM, 2, dtype=F32) / HEAD_DIM))
    ang = jnp.arange(seq).astype(F32)[:, None] * inv_freq[None, :]
    scale = HEAD_DIM ** -0.5
    return jnp.cos(ang) * scale, jnp.sin(ang) * scale


def kernel(x, norm1_g, w_in, w_ret_branch, w_pool_group, pool_scale, w_pool_branch, w_out,
           norm2_g, w_ffn_in, w_ffn_out, norm_final_g):
    batch, seq, d = x.shape
    assert d == D_MODEL and norm1_g.shape[0] == 1 and seq % PROJ_TM == 0
    assert math.log2(HEAD_DIM ** -0.5) == -4.0
    x2d = x.reshape(batch * seq, D_MODEL)
    cos_t, sin_t = _rotary_tables(seq)
    w_in_b = w_in[0].astype(BF16)

    qk, u = _qk_projection(x2d, norm1_g[0][None, :], w_in_b, cos_t, sin_t, seq=seq)
    vg = _vg_projection(u, w_in_b)
    pooled = _pool_projection(u, w_in_b, w_pool_group[0].astype(BF16), pool_scale[0][None, :],
                              seq=seq)
    og = _retention(qk, vg, _retention_tables(), batch=batch, seq=seq)
    merged = _merge(og, pooled, vg, w_ret_branch[0].astype(BF16), w_pool_branch[0].astype(BF16))
    h1 = _out_projection(merged, w_out[0].astype(BF16), x2d)
    out = _ffn(h1, norm2_g[0][None, :], _interleave_ffn_in(w_ffn_in[0], FFN_TF),
               w_ffn_out[0].astype(BF16), norm_final_g[None, :])
    return out.reshape(batch, seq, D_MODEL)
```

```python
import functools
import math

import jax
import jax.numpy as jnp
from jax import lax
from jax.experimental import pallas as pl
from jax.experimental.pallas import tpu as pltpu

D_MODEL = 2048
CHUNK = 64
N_HEADS = 8
HEAD_DIM = 256
D_RET = N_HEADS * HEAD_DIM
POOL_WINDOWS = (2, 4, 8, 16)
POOL_GROUP_DIM = 256
D_POOL = 1024
D_FF = 5632
D_PROJ = 4 * D_RET + D_POOL + 2 * D_MODEL
ROPE_BASE = 10000.0
NORM_EPS = 1e-6
POOL_HALO = 16

VMEM_LIMIT_BYTES = 56 * 1024 * 1024
RET_BLOCK = 256
PROJ_TM = 1024
PROJ_TN = 1024
PROJ_PIECE = 256
FFN_TM = 1024
FFN_TF = 256
MERGE_TM = 256
MERGE_PIECE = 512

F32 = jnp.float32
BF16 = jnp.bfloat16


def _rms_scale(xf):
    return xf * lax.rsqrt(jnp.mean(xf * xf, axis=-1, keepdims=True) + NORM_EPS)


def _sigmoid(x):
    return 0.5 * jnp.tanh(0.5 * x) + 0.5


def _norm_rows_to(dst_refs, src_ref, gain_ref, rows_per_chunk=256):
    n_chunks = src_ref.shape[0] // rows_per_chunk

    def body(c, carry):
        r0 = pl.multiple_of(c * rows_per_chunk, rows_per_chunk)
        xf = src_ref[pl.ds(r0, rows_per_chunk), :]
        u = (_rms_scale(xf) * gain_ref[...]).astype(BF16)
        for dst_ref in dst_refs:
            dst_ref[pl.ds(r0, rows_per_chunk), :] = u
        return carry

    lax.fori_loop(0, n_chunks, body, 0)


def _params():
    return pltpu.CompilerParams(
        dimension_semantics=("arbitrary", "arbitrary"),
        vmem_limit_bytes=VMEM_LIMIT_BYTES)


def _qk_kernel(x_ref, g_ref, w_ref, cos_ref, sin_ref, o_ref, uo_ref, u_ref):
    @pl.when(pl.program_id(1) == 0)
    def _():
        _norm_rows_to((u_ref, uo_ref), x_ref, g_ref)

    acc = jnp.dot(u_ref[...], w_ref[...], preferred_element_type=F32)
    cos = cos_ref[...]
    sin = sin_ref[...]
    half = HEAD_DIM // 2
    for hh in range(acc.shape[1] // HEAD_DIM):
        c0 = hh * HEAD_DIM
        t1 = acc[:, c0:c0 + half]
        t2 = acc[:, c0 + half:c0 + HEAD_DIM]
        o_ref[:, c0:c0 + half] = (t1 * cos - t2 * sin).astype(BF16)
        o_ref[:, c0 + half:c0 + HEAD_DIM] = (t1 * sin + t2 * cos).astype(BF16)


def _qk_projection(x2d, g1, w_in, cos_t, sin_t, *, seq):
    m = x2d.shape[0]
    tm, tn = PROJ_TM, PROJ_TN
    bps = seq // tm
    return pl.pallas_call(
        _qk_kernel,
        grid=(m // tm, 2 * D_RET // tn),
        in_specs=[
            pl.BlockSpec((tm, D_MODEL), lambda i, j: (i, 0)),
            pl.BlockSpec((1, D_MODEL), lambda i, j: (0, 0)),
            pl.BlockSpec((D_MODEL, tn), lambda i, j: (0, j)),
            pl.BlockSpec((tm, HEAD_DIM // 2), lambda i, j: (i % bps, 0)),
            pl.BlockSpec((tm, HEAD_DIM // 2), lambda i, j: (i % bps, 0)),
        ],
        out_specs=[
            pl.BlockSpec((tm, tn), lambda i, j: (i, j)),
            pl.BlockSpec((tm, D_MODEL), lambda i, j: (i, 0)),
        ],
        out_shape=[
            jax.ShapeDtypeStruct((m, 2 * D_RET), BF16),
            jax.ShapeDtypeStruct((m, D_MODEL), BF16),
        ],
        scratch_shapes=[pltpu.VMEM((tm, D_MODEL), BF16)],
        compiler_params=_params(),
        name="qk_projection",
    )(x2d, g1, w_in, cos_t, sin_t)


def _vg_kernel(u_ref, w_ref, o_ref, *, n_v, n_rg):
    j = pl.program_id(1)

    def project(epilogue):
        for c0 in range(0, o_ref.shape[1], PROJ_PIECE):
            cs = slice(c0, c0 + PROJ_PIECE)
            acc = jnp.dot(u_ref[...], w_ref[:, cs], preferred_element_type=F32)
            o_ref[:, cs] = epilogue(acc).astype(BF16)

    @pl.when(j < n_v)
    def _():
        project(lambda acc: acc)

    @pl.when(jnp.logical_and(j >= n_v, j < n_v + n_rg))
    def _():
        project(lambda acc: acc * _sigmoid(acc))

    @pl.when(j >= n_v + n_rg)
    def _():
        project(_sigmoid)


def _vg_projection(u, w_in):
    m = u.shape[0]
    tm, tn = PROJ_TM, PROJ_TN
    n_v = D_RET // tn
    n_rg = D_RET // tn
    n_gate = 2 * D_MODEL // tn
    w_v0 = 2 * D_RET // tn
    w_gate_shift = D_POOL // tn
    kern = functools.partial(_vg_kernel, n_v=n_v, n_rg=n_rg)

    def w_map(i, j):
        return (0, w_v0 + j + jnp.where(j >= n_v + n_rg, w_gate_shift, 0))

    return pl.pallas_call(
        kern,
        grid=(m // tm, n_v + n_rg + n_gate),
        in_specs=[
            pl.BlockSpec((tm, D_MODEL), lambda i, j: (i, 0)),
            pl.BlockSpec((D_MODEL, tn), w_map),
        ],
        out_specs=pl.BlockSpec((tm, tn), lambda i, j: (i, j)),
        out_shape=jax.ShapeDtypeStruct((m, 2 * D_RET + 2 * D_MODEL), BF16),
        compiler_params=_params(),
        name="vg_projection",
    )(u, w_in)


def _pool_kernel(u_ref, w_ref, wpg_ref, pscale_ref, o_ref, halo_ref, ext_ref, *, bps):
    tm = u_ref.shape[0]
    seq_block = pl.program_id(0) % bps
    acc = jnp.dot(u_ref[...], w_ref[...], preferred_element_type=F32)
    prev = halo_ref[...]
    ext_ref[0:POOL_HALO, :] = jnp.where(seq_block == 0, jnp.zeros_like(prev), prev)
    ext_ref[POOL_HALO:POOL_HALO + tm, :] = acc
    halo_ref[...] = acc[tm - POOL_HALO:, :]
    pos = seq_block * tm + lax.broadcasted_iota(jnp.int32, (tm, 1), 0)
    for grp, win in enumerate(POOL_WINDOWS):
        cs = slice(grp * POOL_GROUP_DIM, (grp + 1) * POOL_GROUP_DIM)
        s = ext_ref[POOL_HALO:POOL_HALO + tm, cs]
        for d in range(1, win):
            s = s + ext_ref[POOL_HALO - d:POOL_HALO - d + tm, cs]
        inv_cnt = 1.0 / jnp.minimum(pos + 1, win).astype(F32)
        pooled = s * inv_cnt - acc[:, cs]
        y = jnp.dot(pooled.astype(BF16), wpg_ref[grp], preferred_element_type=F32)
        o_ref[:, cs] = (y * pscale_ref[:, cs]).astype(BF16)


def _pool_projection(u, w_in, wpg, pscale, *, seq):
    m = u.shape[0]
    tm = PROJ_TM
    kern = functools.partial(_pool_kernel, bps=seq // tm)
    return pl.pallas_call(
        kern,
        grid=(m // tm,),
        in_specs=[
            pl.BlockSpec((tm, D_MODEL), lambda i: (i, 0)),
            pl.BlockSpec((D_MODEL, D_POOL), lambda i: (0, 4 * D_RET // D_POOL)),
            pl.BlockSpec((len(POOL_WINDOWS), POOL_GROUP_DIM, POOL_GROUP_DIM), lambda i: (0, 0, 0)),
            pl.BlockSpec((1, D_POOL), lambda i: (0, 0)),
        ],
        out_specs=pl.BlockSpec((tm, D_POOL), lambda i: (i, 0)),
        out_shape=jax.ShapeDtypeStruct((m, D_POOL), BF16),
        scratch_shapes=[
            pltpu.VMEM((POOL_HALO, D_POOL), F32),
            pltpu.VMEM((tm + POOL_HALO, D_POOL), F32),
        ],
        compiler_params=pltpu.CompilerParams(
            dimension_semantics=("arbitrary",), vmem_limit_bytes=VMEM_LIMIT_BYTES),
        name="pool_projection",
    )(u, w_in, wpg, pscale)


def _retention_kernel(q_ref, k_ref, v_ref, rg_ref, dmask_ref, qdec_ref, kdec_ref, bdec_ref,
                      o_ref, state_ref):
    @pl.when(pl.program_id(1) == 0)
    def _():
        state_ref[...] = jnp.zeros_like(state_ref)

    for h in range(N_HEADS):
        cs = slice(h * HEAD_DIM, (h + 1) * HEAD_DIM)
        q = q_ref[:, cs]
        k = k_ref[:, cs]
        v = v_ref[:, cs]
        scores = lax.dot_general(q, k, (((1,), (1,)), ((), ())), preferred_element_type=F32)
        p = (scores * dmask_ref[h]).astype(BF16)
        o = jnp.dot(p, v, preferred_element_type=F32)
        st = state_ref[h]
        o = o + jnp.dot(q, st.astype(BF16), preferred_element_type=F32) * qdec_ref[h]
        kd = (k.astype(F32) * kdec_ref[h]).astype(BF16)
        upd = lax.dot_general(kd, v, (((0,), (0,)), ((), ())), preferred_element_type=F32)
        state_ref[h] = st * bdec_ref[h] + upd
        o_ref[:, cs] = (_rms_scale(o) * rg_ref[:, cs].astype(F32)).astype(BF16)


def _retention(qk, vg, tables, *, batch, seq):
    dmask, qdec, kdec, bdec = tables
    blk = RET_BLOCK
    nblk = seq // blk

    def seg(c):
        return pl.BlockSpec((blk, D_RET), lambda b, t: (b * nblk + t, c))

    tab = pl.BlockSpec((N_HEADS, blk, HEAD_DIM), lambda b, t: (0, 0, 0))
    return pl.pallas_call(
        _retention_kernel,
        grid=(batch, nblk),
        in_specs=[seg(0), seg(1), seg(0), seg(1), tab, tab, tab,
                  pl.BlockSpec((N_HEADS, 1, HEAD_DIM), lambda b, t: (0, 0, 0))],
        out_specs=pl.BlockSpec((blk, D_RET), lambda b, t: (b * nblk + t, 0)),
        out_shape=jax.ShapeDtypeStruct((batch * seq, D_RET), BF16),
        scratch_shapes=[pltpu.VMEM((N_HEADS, HEAD_DIM, HEAD_DIM), F32)],
        compiler_params=_params(),
        name="retention",
    )(qk, qk, vg, vg, dmask, qdec, kdec, bdec)


def _retention_tables():
    blk = RET_BLOCK
    log_g = jnp.log(1.0 - 2.0 ** (-5.0 - jnp.arange(N_HEADS, dtype=F32)))
    n = jnp.arange(blk, dtype=F32)
    dist = jnp.abs(n[:, None] - n[None, :])
    chunk = jnp.arange(blk) // CHUNK
    visible = (chunk[None, :] <= chunk[:, None]).astype(F32)
    dmask = jnp.exp(log_g[:, None, None] * dist[None]) * visible[None]
    qdec = jnp.exp(log_g[:, None] * (n[None, :] + 1.0))
    kdec = jnp.exp(log_g[:, None] * (blk - 1.0 - n[None, :]))
    bdec = jnp.exp(log_g * blk)
    qdec = jnp.broadcast_to(qdec[:, :, None], (N_HEADS, blk, HEAD_DIM))
    kdec = jnp.broadcast_to(kdec[:, :, None], (N_HEADS, blk, HEAD_DIM))
    bdec = jnp.broadcast_to(bdec[:, None, None], (N_HEADS, 1, HEAD_DIM))
    return dmask, qdec, kdec, bdec


def _merge_out_kernel(og_ref, pool_ref, gate_ref, x_ref, wr_ref, wp_ref, wo_ref, o_ref, m_ref):
    og = og_ref[...]
    pool = pool_ref[...]
    for c0 in range(0, D_MODEL, MERGE_PIECE):
        cs = slice(c0, c0 + MERGE_PIECE)
        y_ret = jnp.dot(og, wr_ref[:, cs], preferred_element_type=F32)
        y_pool = jnp.dot(pool, wp_ref[:, cs], preferred_element_type=F32)
        g_ret = gate_ref[:, cs].astype(F32)
        g_pool = gate_ref[:, D_MODEL + c0:D_MODEL + c0 + MERGE_PIECE].astype(F32)
        m_ref[:, cs] = (g_ret * y_ret + g_pool * y_pool).astype(BF16)
    merged = m_ref[...]
    for c0 in range(0, D_MODEL, MERGE_PIECE):
        cs = slice(c0, c0 + MERGE_PIECE)
        o_ref[:, cs] = x_ref[:, cs] + jnp.dot(merged, wo_ref[:, cs], preferred_element_type=F32)


def _merge_out(og, pooled, vg, x2d, w_ret, w_pool, w_out, *, tm=MERGE_TM):
    m = og.shape[0]

    def full(shape):
        return pl.BlockSpec(shape, lambda i: (0, 0))

    return pl.pallas_call(
        _merge_out_kernel,
        grid=(m // tm,),
        in_specs=[
            pl.BlockSpec((tm, D_RET), lambda i: (i, 0)),
            pl.BlockSpec((tm, D_POOL), lambda i: (i, 0)),
            pl.BlockSpec((tm, 2 * D_MODEL), lambda i: (i, 1)),
            pl.BlockSpec((tm, D_MODEL), lambda i: (i, 0)),
            full((D_RET, D_MODEL)), full((D_POOL, D_MODEL)), full((D_MODEL, D_MODEL)),
        ],
        out_specs=pl.BlockSpec((tm, D_MODEL), lambda i: (i, 0)),
        out_shape=jax.ShapeDtypeStruct((m, D_MODEL), F32),
        scratch_shapes=[pltpu.VMEM((tm, D_MODEL), BF16)],
        compiler_params=pltpu.CompilerParams(
            dimension_semantics=("arbitrary",), vmem_limit_bytes=VMEM_LIMIT_BYTES),
        name="merge_out_projection",
    )(og, pooled, vg, x2d, w_ret, w_pool, w_out)


def _ffn_kernel(h_ref, g2_ref, wa_ref, wb_ref, wo_ref, gf_ref, o_ref, u_ref, *, n_ff):
    j = pl.program_id(1)

    @pl.when(j == 0)
    def _():
        _norm_rows_to((u_ref,), h_ref, g2_ref)
        o_ref[...] = h_ref[...]

    u = u_ref[...]
    a = jnp.dot(u, wa_ref[...], preferred_element_type=F32)
    b = jnp.dot(u, wb_ref[...], preferred_element_type=F32)
    gated = (a * _sigmoid(a) * b).astype(BF16)
    o_ref[...] += jnp.dot(gated, wo_ref[...], preferred_element_type=F32)

    @pl.when(j == n_ff - 1)
    def _():
        o_ref[...] = _rms_scale(o_ref[...]) * gf_ref[...]


def _ffn(h1, g2, w_ffn_in, w_ffn_out, g_final, *, tm=FFN_TM, tf=FFN_TF):
    m = h1.shape[0]
    n_ff = D_FF // tf
    kern = functools.partial(_ffn_kernel, n_ff=n_ff)
    return pl.pallas_call(
        kern,
        grid=(m // tm, n_ff),
        in_specs=[
            pl.BlockSpec((tm, D_MODEL), lambda i, j: (i, 0)),
            pl.BlockSpec((1, D_MODEL), lambda i, j: (0, 0)),
            pl.BlockSpec((D_MODEL, tf), lambda i, j: (0, j)),
            pl.BlockSpec((D_MODEL, tf), lambda i, j: (0, n_ff + j)),
            pl.BlockSpec((tf, D_MODEL), lambda i, j: (j, 0)),
            pl.BlockSpec((1, D_MODEL), lambda i, j: (0, 0)),
        ],
        out_specs=pl.BlockSpec((tm, D_MODEL), lambda i, j: (i, 0)),
        out_shape=jax.ShapeDtypeStruct((m, D_MODEL), F32),
        scratch_shapes=[pltpu.VMEM((tm, D_MODEL), BF16)],
        compiler_params=_params(),
        name="swiglu_ffn",
    )(h1, g2, w_ffn_in, w_ffn_in, w_ffn_out, g_final)


def _rotary_tables(seq):
    inv_freq = 1.0 / (ROPE_BASE ** (jnp.arange(0, HEAD_DIM, 2, dtype=F32) / HEAD_DIM))
    ang = jnp.arange(seq).astype(F32)[:, None] * inv_freq[None, :]
    scale = HEAD_DIM ** -0.5
    return jnp.cos(ang) * scale, jnp.sin(ang) * scale


def kernel(x, norm1_g, w_in, w_ret_branch, w_pool_group, pool_scale, w_pool_branch, w_out,
           norm2_g, w_ffn_in, w_ffn_out, norm_final_g):
    batch, seq, d = x.shape
    assert d == D_MODEL and norm1_g.shape[0] == 1 and seq % PROJ_TM == 0
    assert math.log2(HEAD_DIM ** -0.5) == -4.0
    x2d = x.reshape(batch * seq, D_MODEL)
    cos_t, sin_t = _rotary_tables(seq)
    w_in_b = w_in[0].astype(BF16)

    qk, u = _qk_projection(x2d, norm1_g[0][None, :], w_in_b, cos_t, sin_t, seq=seq)
    vg = _vg_projection(u, w_in_b)
    pooled = _pool_projection(u, w_in_b, w_pool_group[0].astype(BF16), pool_scale[0][None, :],
                              seq=seq)
    og = _retention(qk, vg, _retention_tables(), batch=batch, seq=seq)
    h1 = _merge_out(og, pooled, vg, x2d, w_ret_branch[0].astype(BF16),
                    w_pool_branch[0].astype(BF16), w_out[0].astype(BF16))
    out = _ffn(h1, norm2_g[0][None, :], w_ffn_in[0].astype(BF16), w_ffn_out[0].astype(BF16),
               norm_final_g[None, :])
    return out.reshape(batch, seq, D_MODEL)
```

```python
import functools
import math

import jax
import jax.numpy as jnp
from jax import lax
from jax.experimental import pallas as pl
from jax.experimental.pallas import tpu as pltpu

D_MODEL = 2048
CHUNK = 64
N_HEADS = 8
HEAD_DIM = 256
D_RET = N_HEADS * HEAD_DIM
POOL_WINDOWS = (2, 4, 8, 16)
POOL_GROUP_DIM = 256
D_POOL = 1024
D_FF = 5632
D_PROJ = 4 * D_RET + D_POOL + 2 * D_MODEL
ROPE_BASE = 10000.0
NORM_EPS = 1e-6
POOL_HALO = 16
assert all(w & (w - 1) == 0 and w <= POOL_HALO for w in POOL_WINDOWS)

VMEM_LIMIT_BYTES = 56 * 1024 * 1024
RET_BLOCK = 256
PROJ_TM = 1024
PROJ_TN = 1024
PROJ_PIECE = 256
FFN_TM = 1024
FFN_TF = 512
MERGE_TM = 256
MERGE_PIECE = 512

F32 = jnp.float32
BF16 = jnp.bfloat16


def _rms_scale(xf):
    return xf * lax.rsqrt(jnp.mean(xf * xf, axis=-1, keepdims=True) + NORM_EPS)


def _sigmoid(x):
    return 0.5 * jnp.tanh(0.5 * x) + 0.5


def _norm_rows_to(dst_refs, src_ref, gain_ref, rows_per_chunk=256):
    n_chunks = src_ref.shape[0] // rows_per_chunk

    def body(c, carry):
        r0 = pl.multiple_of(c * rows_per_chunk, rows_per_chunk)
        xf = src_ref[pl.ds(r0, rows_per_chunk), :]
        u = (_rms_scale(xf) * gain_ref[...]).astype(BF16)
        for dst_ref in dst_refs:
            dst_ref[pl.ds(r0, rows_per_chunk), :] = u
        return carry

    lax.fori_loop(0, n_chunks, body, 0)


def _params():
    return pltpu.CompilerParams(
        dimension_semantics=("arbitrary", "arbitrary"),
        vmem_limit_bytes=VMEM_LIMIT_BYTES)


def _qk_kernel(x_ref, g_ref, w_ref, cos_ref, sin_ref, dec_ref, o_ref, uo_ref, u_ref):
    @pl.when(pl.program_id(1) == 0)
    def _():
        _norm_rows_to((u_ref, uo_ref), x_ref, g_ref)

    half = HEAD_DIM // 2
    tm = o_ref.shape[0]
    for hh in range(o_ref.shape[1] // HEAD_DIM):
        c0 = hh * HEAD_DIM
        acc = jnp.dot(u_ref[...], w_ref[:, c0:c0 + HEAD_DIM], preferred_element_type=F32)
        for r0 in range(0, tm, RET_BLOCK):
            rs = slice(r0, r0 + RET_BLOCK)
            t1 = acc[rs, :half]
            t2 = acc[rs, half:]
            cos = cos_ref[rs, :]
            sin = sin_ref[rs, :]
            dec = dec_ref[hh]
            o_ref[rs, c0:c0 + half] = ((t1 * cos - t2 * sin) * dec).astype(BF16)
            o_ref[rs, c0 + half:c0 + HEAD_DIM] = ((t1 * sin + t2 * cos) * dec).astype(BF16)


def _qk_projection(x2d, g1, w_in, cos_t, sin_t, row_decay, *, seq):
    m = x2d.shape[0]
    tm, tn = PROJ_TM, PROJ_TN
    bps = seq // tm
    return pl.pallas_call(
        _qk_kernel,
        grid=(m // tm, 2 * D_RET // tn),
        in_specs=[
            pl.BlockSpec((tm, D_MODEL), lambda i, j: (i, 0)),
            pl.BlockSpec((1, D_MODEL), lambda i, j: (0, 0)),
            pl.BlockSpec((D_MODEL, tn), lambda i, j: (0, j)),
            pl.BlockSpec((tm, HEAD_DIM // 2), lambda i, j: (i % bps, 0)),
            pl.BlockSpec((tm, HEAD_DIM // 2), lambda i, j: (i % bps, 0)),
            pl.BlockSpec((tn // HEAD_DIM, RET_BLOCK, HEAD_DIM // 2), lambda i, j: (j, 0, 0)),
        ],
        out_specs=[
            pl.BlockSpec((tm, tn), lambda i, j: (i, j)),
            pl.BlockSpec((tm, D_MODEL), lambda i, j: (i, 0)),
        ],
        out_shape=[
            jax.ShapeDtypeStruct((m, 2 * D_RET), BF16),
            jax.ShapeDtypeStruct((m, D_MODEL), BF16),
        ],
        scratch_shapes=[pltpu.VMEM((tm, D_MODEL), BF16)],
        compiler_params=_params(),
        name="qk_projection",
    )(x2d, g1, w_in, cos_t, sin_t, row_decay)


def _vg_kernel(u_ref, w_ref, o_ref, *, n_v, n_rg):
    j = pl.program_id(1)

    def project(epilogue):
        for c0 in range(0, o_ref.shape[1], PROJ_PIECE):
            cs = slice(c0, c0 + PROJ_PIECE)
            acc = jnp.dot(u_ref[...], w_ref[:, cs], preferred_element_type=F32)
            o_ref[:, cs] = epilogue(acc).astype(BF16)

    @pl.when(j < n_v)
    def _():
        project(lambda acc: acc)

    @pl.when(jnp.logical_and(j >= n_v, j < n_v + n_rg))
    def _():
        project(lambda acc: acc * _sigmoid(acc))

    @pl.when(j >= n_v + n_rg)
    def _():
        project(_sigmoid)


def _vg_projection(u, w_in):
    m = u.shape[0]
    tm, tn = PROJ_TM, PROJ_TN
    n_v = D_RET // tn
    n_rg = D_RET // tn
    n_gate = 2 * D_MODEL // tn
    w_v0 = 2 * D_RET // tn
    w_gate_shift = D_POOL // tn
    kern = functools.partial(_vg_kernel, n_v=n_v, n_rg=n_rg)

    def w_map(i, j):
        return (0, w_v0 + j + jnp.where(j >= n_v + n_rg, w_gate_shift, 0))

    return pl.pallas_call(
        kern,
        grid=(m // tm, n_v + n_rg + n_gate),
        in_specs=[
            pl.BlockSpec((tm, D_MODEL), lambda i, j: (i, 0)),
            pl.BlockSpec((D_MODEL, tn), w_map),
        ],
        out_specs=pl.BlockSpec((tm, tn), lambda i, j: (i, j)),
        out_shape=jax.ShapeDtypeStruct((m, 2 * D_RET + 2 * D_MODEL), BF16),
        compiler_params=_params(),
        name="vg_projection",
    )(u, w_in)


def _pool_kernel(u_ref, w_ref, wpg_ref, pscale_ref, o_ref, halo_ref, p_ref, a_ref, b_ref, *, bps):
    tm = u_ref.shape[0]
    hl = POOL_HALO
    n = tm + hl
    seq_block = pl.program_id(0) % bps
    pos = seq_block * tm + lax.broadcasted_iota(jnp.int32, (tm, 1), 0)
    zeros = jnp.zeros((hl, D_POOL), F32)
    p_ref[0:hl, :] = zeros
    a_ref[0:hl, :] = zeros
    b_ref[0:hl, :] = zeros
    for grp in reversed(range(len(POOL_WINDOWS))):
        win = POOL_WINDOWS[grp]
        cs = slice(grp * POOL_GROUP_DIM, (grp + 1) * POOL_GROUP_DIM)
        if grp % 2 == 1:
            ps = slice((grp - 1) * POOL_GROUP_DIM, (grp + 1) * POOL_GROUP_DIM)
            acc2 = jnp.dot(u_ref[...], w_ref[:, ps], preferred_element_type=F32)
            prev = halo_ref[:, ps]
            p_ref[hl:2 * hl, ps] = jnp.where(seq_block == 0, jnp.zeros_like(prev), prev)
            p_ref[2 * hl:2 * hl + tm, ps] = acc2
            halo_ref[:, ps] = acc2[tm - hl:, :]
        src = p_ref
        for lvl in range(win.bit_length() - 1):
            d = 1 << lvl
            dst = (a_ref, b_ref)[lvl % 2]
            dst[hl:hl + n, cs] = src[hl:hl + n, cs] + src[hl - d:hl - d + n, cs]
            src = dst
        s = src[2 * hl:2 * hl + tm, cs]
        inv_cnt = 1.0 / jnp.minimum(pos + 1, win).astype(F32)
        pooled = s * inv_cnt - p_ref[2 * hl:2 * hl + tm, cs]
        y = jnp.dot(pooled.astype(BF16), wpg_ref[grp], preferred_element_type=F32)
        o_ref[:, cs] = (y * pscale_ref[:, cs]).astype(BF16)


def _pool_projection(u, w_in, wpg, pscale, *, seq):
    m = u.shape[0]
    tm = PROJ_TM
    kern = functools.partial(_pool_kernel, bps=seq // tm)
    return pl.pallas_call(
        kern,
        grid=(m // tm,),
        in_specs=[
            pl.BlockSpec((tm, D_MODEL), lambda i: (i, 0)),
            pl.BlockSpec((D_MODEL, D_POOL), lambda i: (0, 4 * D_RET // D_POOL)),
            pl.BlockSpec((len(POOL_WINDOWS), POOL_GROUP_DIM, POOL_GROUP_DIM), lambda i: (0, 0, 0)),
            pl.BlockSpec((1, D_POOL), lambda i: (0, 0)),
        ],
        out_specs=pl.BlockSpec((tm, D_POOL), lambda i: (i, 0)),
        out_shape=jax.ShapeDtypeStruct((m, D_POOL), BF16),
        scratch_shapes=[pltpu.VMEM((POOL_HALO, D_POOL), F32)]
        + [pltpu.VMEM((tm + 2 * POOL_HALO, D_POOL), F32)] * 3,
        compiler_params=pltpu.CompilerParams(
            dimension_semantics=("arbitrary",), vmem_limit_bytes=VMEM_LIMIT_BYTES),
        name="pool_projection",
    )(u, w_in, wpg, pscale)


def _retention_kernel(q_ref, k_ref, v_ref, rg_ref, dmask_ref, bdec_ref, o_ref, state_ref):
    @pl.when(pl.program_id(1) == 0)
    def _():
        state_ref[...] = jnp.zeros_like(state_ref)

    for h in range(N_HEADS):
        cs = slice(h * HEAD_DIM, (h + 1) * HEAD_DIM)
        q = q_ref[:, cs]
        k = k_ref[:, cs]
        v = v_ref[:, cs]
        scores = lax.dot_general(q, k, (((1,), (1,)), ((), ())), preferred_element_type=F32)
        p = (scores * dmask_ref[h]).astype(BF16)
        st = state_ref[h]
        o = (jnp.dot(p, v, preferred_element_type=F32)
             + jnp.dot(q, st.astype(BF16), preferred_element_type=F32))
        upd = lax.dot_general(k, v, (((0,), (0,)), ((), ())), preferred_element_type=F32)
        state_ref[h] = st * bdec_ref[h] + upd
        o_ref[:, cs] = (_rms_scale(o) * rg_ref[:, cs].astype(F32)).astype(BF16)


def _retention(qk, vg, dmask, bdec, *, batch, seq):
    blk = RET_BLOCK
    nblk = seq // blk

    def seg(c):
        return pl.BlockSpec((blk, D_RET), lambda b, t: (b * nblk + t, c))

    return pl.pallas_call(
        _retention_kernel,
        grid=(batch, nblk),
        in_specs=[seg(0), seg(1), seg(0), seg(1),
                  pl.BlockSpec((N_HEADS, blk, blk), lambda b, t: (0, 0, 0)),
                  pl.BlockSpec((N_HEADS, 1, HEAD_DIM), lambda b, t: (0, 0, 0))],
        out_specs=pl.BlockSpec((blk, D_RET), lambda b, t: (b * nblk + t, 0)),
        out_shape=jax.ShapeDtypeStruct((batch * seq, D_RET), BF16),
        scratch_shapes=[pltpu.VMEM((N_HEADS, HEAD_DIM, HEAD_DIM), F32)],
        compiler_params=_params(),
        name="retention",
    )(qk, qk, vg, vg, dmask, bdec)


def _retention_tables():
    blk = RET_BLOCK
    log_g = jnp.log(1.0 - 2.0 ** (-5.0 - jnp.arange(N_HEADS, dtype=F32)))
    n = jnp.arange(blk, dtype=F32)
    diff = n[:, None] - n[None, :]
    chunk = jnp.arange(blk) // CHUNK
    visible = (chunk[None, :] <= chunk[:, None]).astype(F32)
    dmask = jnp.exp(log_g[:, None, None] * (jnp.abs(diff) - (blk + diff))[None]) * visible[None]
    qdec = jnp.exp(log_g[:, None] * (n[None, :] + 1.0))
    kdec = jnp.exp(log_g[:, None] * (blk - 1.0 - n[None, :]))
    row_decay = jnp.concatenate([qdec, kdec], axis=0)
    row_decay = jnp.broadcast_to(row_decay[:, :, None], (2 * N_HEADS, blk, HEAD_DIM // 2))
    bdec = jnp.exp(log_g * blk)
    bdec = jnp.broadcast_to(bdec[:, None, None], (N_HEADS, 1, HEAD_DIM))
    return dmask, row_decay, bdec


def _merge_out_kernel(og_ref, pool_ref, gate_ref, x_ref, wr_ref, wp_ref, wo_ref, o_ref, m_ref):
    og = og_ref[...]
    pool = pool_ref[...]
    for c0 in range(0, D_MODEL, MERGE_PIECE):
        cs = slice(c0, c0 + MERGE_PIECE)
        y_ret = jnp.dot(og, wr_ref[:, cs], preferred_element_type=F32)
        y_pool = jnp.dot(pool, wp_ref[:, cs], preferred_element_type=F32)
        g_ret = gate_ref[:, cs].astype(F32)
        g_pool = gate_ref[:, D_MODEL + c0:D_MODEL + c0 + MERGE_PIECE].astype(F32)
        m_ref[:, cs] = (g_ret * y_ret + g_pool * y_pool).astype(BF16)
    merged = m_ref[...]
    for c0 in range(0, D_MODEL, MERGE_PIECE):
        cs = slice(c0, c0 + MERGE_PIECE)
        o_ref[:, cs] = x_ref[:, cs] + jnp.dot(merged, wo_ref[:, cs], preferred_element_type=F32)


def _merge_out(og, pooled, vg, x2d, w_ret, w_pool, w_out, *, tm=MERGE_TM):
    m = og.shape[0]

    def full(shape):
        return pl.BlockSpec(shape, lambda i: (0, 0))

    return pl.pallas_call(
        _merge_out_kernel,
        grid=(m // tm,),
        in_specs=[
            pl.BlockSpec((tm, D_RET), lambda i: (i, 0)),
            pl.BlockSpec((tm, D_POOL), lambda i: (i, 0)),
            pl.BlockSpec((tm, 2 * D_MODEL), lambda i: (i, 1)),
            pl.BlockSpec((tm, D_MODEL), lambda i: (i, 0)),
            full((D_RET, D_MODEL)), full((D_POOL, D_MODEL)), full((D_MODEL, D_MODEL)),
        ],
        out_specs=pl.BlockSpec((tm, D_MODEL), lambda i: (i, 0)),
        out_shape=jax.ShapeDtypeStruct((m, D_MODEL), F32),
        scratch_shapes=[pltpu.VMEM((tm, D_MODEL), BF16)],
        compiler_params=pltpu.CompilerParams(
            dimension_semantics=("arbitrary",), vmem_limit_bytes=VMEM_LIMIT_BYTES),
        name="merge_out_projection",
    )(og, pooled, vg, x2d, w_ret, w_pool, w_out)


def _ffn_kernel(h_ref, g2_ref, wa_ref, wb_ref, wo_ref, gf_ref, o_ref, u_ref, *, n_ff):
    j = pl.program_id(1)

    @pl.when(j == 0)
    def _():
        _norm_rows_to((u_ref,), h_ref, g2_ref)
        o_ref[...] = h_ref[...]

    u = u_ref[...]
    a = jnp.dot(u, wa_ref[...], preferred_element_type=F32)
    b = jnp.dot(u, wb_ref[...], preferred_element_type=F32)
    gated = (a * _sigmoid(a) * b).astype(BF16)
    o_ref[...] += jnp.dot(gated, wo_ref[...], preferred_element_type=F32)

    @pl.when(j == n_ff - 1)
    def _():
        o_ref[...] = _rms_scale(o_ref[...]) * gf_ref[...]


def _ffn(h1, g2, w_ffn_in, w_ffn_out, g_final, *, tm=FFN_TM, tf=FFN_TF):
    m = h1.shape[0]
    n_ff = D_FF // tf
    kern = functools.partial(_ffn_kernel, n_ff=n_ff)
    return pl.pallas_call(
        kern,
        grid=(m // tm, n_ff),
        in_specs=[
            pl.BlockSpec((tm, D_MODEL), lambda i, j: (i, 0)),
            pl.BlockSpec((1, D_MODEL), lambda i, j: (0, 0)),
            pl.BlockSpec((D_MODEL, tf), lambda i, j: (0, j)),
            pl.BlockSpec((D_MODEL, tf), lambda i, j: (0, n_ff + j)),
            pl.BlockSpec((tf, D_MODEL), lambda i, j: (j, 0)),
            pl.BlockSpec((1, D_MODEL), lambda i, j: (0, 0)),
        ],
        out_specs=pl.BlockSpec((tm, D_MODEL), lambda i, j: (i, 0)),
        out_shape=jax.ShapeDtypeStruct((m, D_MODEL), F32),
        scratch_shapes=[pltpu.VMEM((tm, D_MODEL), BF16)],
        compiler_params=_params(),
        name="swiglu_ffn",
    )(h1, g2, w_ffn_in, w_ffn_in, w_ffn_out, g_final)


def _rotary_tables(seq):
    inv_freq = 1.0 / (ROPE_BASE ** (jnp.arange(0, HEAD_DIM, 2, dtype=F32) / HEAD_DIM))
    ang = jnp.arange(seq).astype(F32)[:, None] * inv_freq[None, :]
    scale = HEAD_DIM ** -0.5
    return jnp.cos(ang) * scale, jnp.sin(ang) * scale


def kernel(x, norm1_g, w_in, w_ret_branch, w_pool_group, pool_scale, w_pool_branch, w_out,
           norm2_g, w_ffn_in, w_ffn_out, norm_final_g):
    batch, seq, d = x.shape
    assert d == D_MODEL and norm1_g.shape[0] == 1 and seq % PROJ_TM == 0
    assert math.log2(HEAD_DIM ** -0.5) == -4.0
    x2d = x.reshape(batch * seq, D_MODEL)
    cos_t, sin_t = _rotary_tables(seq)
    w_in_b = w_in[0].astype(BF16)

    dmask, row_decay, bdec = _retention_tables()
    qk, u = _qk_projection(x2d, norm1_g[0][None, :], w_in_b, cos_t, sin_t, row_decay, seq=seq)
    vg = _vg_projection(u, w_in_b)
    pooled = _pool_projection(u, w_in_b, w_pool_group[0].astype(BF16), pool_scale[0][None, :],
                              seq=seq)
    og = _retention(qk, vg, dmask, bdec, batch=batch, seq=seq)
    h1 = _merge_out(og, pooled, vg, x2d, w_ret_branch[0].astype(BF16),
                    w_pool_branch[0].astype(BF16), w_out[0].astype(BF16))
    out = _ffn(h1, norm2_g[0][None, :], w_ffn_in[0].astype(BF16), w_ffn_out[0].astype(BF16),
               norm_final_g[None, :])
    return out.reshape(batch, seq, D_MODEL)
```

```python
import functools
import math

import jax
import jax.numpy as jnp
from jax import lax
from jax.experimental import pallas as pl
from jax.experimental.pallas import tpu as pltpu

D_MODEL = 2048
CHUNK = 64
N_HEADS = 8
HEAD_DIM = 256
D_RET = N_HEADS * HEAD_DIM
POOL_WINDOWS = (2, 4, 8, 16)
POOL_GROUP_DIM = 256
D_POOL = 1024
D_FF = 5632
D_PROJ = 4 * D_RET + D_POOL + 2 * D_MODEL
ROPE_BASE = 10000.0
NORM_EPS = 1e-6
POOL_HALO = 16
assert all(w & (w - 1) == 0 and w <= POOL_HALO for w in POOL_WINDOWS)

VMEM_LIMIT_BYTES = 56 * 1024 * 1024
RET_BLOCK = 256
PROJ_TM = 1024
PROJ_TN = 1024
PROJ_PIECE = 256
FFN_TM = 1024
FFN_TF = 512
MERGE_PIECE = 512

F32 = jnp.float32
BF16 = jnp.bfloat16
BF16_SUBLANES = 16


def _rms_scale(xf):
    return xf * lax.rsqrt(jnp.mean(xf * xf, axis=-1, keepdims=True) + NORM_EPS)


def _sigmoid(x):
    return 0.5 * jnp.tanh(0.5 * x) + 0.5


def _norm_rows_to(dst_refs, src_ref, gain_ref, rows_per_chunk=256):
    n_chunks = src_ref.shape[0] // rows_per_chunk

    def body(c, carry):
        r0 = pl.multiple_of(c * rows_per_chunk, rows_per_chunk)
        xf = src_ref[pl.ds(r0, rows_per_chunk), :]
        u = (_rms_scale(xf) * gain_ref[...]).astype(BF16)
        for dst_ref in dst_refs:
            dst_ref[pl.ds(r0, rows_per_chunk), :] = u
        return carry

    lax.fori_loop(0, n_chunks, body, 0)


def _params():
    return pltpu.CompilerParams(
        dimension_semantics=("arbitrary", "arbitrary"),
        vmem_limit_bytes=VMEM_LIMIT_BYTES)


def _qk_kernel(x_ref, g_ref, w_ref, cos_ref, sin_ref, dec_ref, o_ref, uo_ref, u_ref):
    @pl.when(pl.program_id(1) == 0)
    def _():
        _norm_rows_to((u_ref, uo_ref), x_ref, g_ref)

    half = HEAD_DIM // 2
    tm = o_ref.shape[0]
    for hh in range(o_ref.shape[1] // HEAD_DIM):
        c0 = hh * HEAD_DIM
        acc = jnp.dot(u_ref[...], w_ref[:, c0:c0 + HEAD_DIM], preferred_element_type=F32)
        for r0 in range(0, tm, RET_BLOCK):
            rs = slice(r0, r0 + RET_BLOCK)
            t1 = acc[rs, :half]
            t2 = acc[rs, half:]
            cos = cos_ref[rs, :]
            sin = sin_ref[rs, :]
            dec = dec_ref[hh]
            o_ref[rs, c0:c0 + half] = ((t1 * cos - t2 * sin) * dec).astype(BF16)
            o_ref[rs, c0 + half:c0 + HEAD_DIM] = ((t1 * sin + t2 * cos) * dec).astype(BF16)


def _qk_projection(x2d, g1, w_in, cos_t, sin_t, row_decay, *, seq):
    m = x2d.shape[0]
    tm, tn = PROJ_TM, PROJ_TN
    bps = seq // tm
    return pl.pallas_call(
        _qk_kernel,
        grid=(m // tm, 2 * D_RET // tn),
        in_specs=[
            pl.BlockSpec((tm, D_MODEL), lambda i, j: (i, 0)),
            pl.BlockSpec((1, D_MODEL), lambda i, j: (0, 0)),
            pl.BlockSpec((D_MODEL, tn), lambda i, j: (0, j)),
            pl.BlockSpec((tm, HEAD_DIM // 2), lambda i, j: (i % bps, 0)),
            pl.BlockSpec((tm, HEAD_DIM // 2), lambda i, j: (i % bps, 0)),
            pl.BlockSpec((tn // HEAD_DIM, RET_BLOCK, HEAD_DIM // 2), lambda i, j: (j, 0, 0)),
        ],
        out_specs=[
            pl.BlockSpec((tm, tn), lambda i, j: (i, j)),
            pl.BlockSpec((tm, D_MODEL), lambda i, j: (i, 0)),
        ],
        out_shape=[
            jax.ShapeDtypeStruct((m, 2 * D_RET), BF16),
            jax.ShapeDtypeStruct((m, D_MODEL), BF16),
        ],
        scratch_shapes=[pltpu.VMEM((tm, D_MODEL), BF16)],
        compiler_params=_params(),
        name="qk_projection",
    )(x2d, g1, w_in, cos_t, sin_t, row_decay)


def _vg_kernel(u_ref, w_ref, *refs, n_v, n_rg, n_cast):
    j = pl.program_id(1)
    cast_in = refs[:n_cast]
    o_ref = refs[n_cast]
    cast_out = refs[n_cast + 1:]

    def project(epilogue):
        for c0 in range(0, o_ref.shape[1], PROJ_PIECE):
            cs = slice(c0, c0 + PROJ_PIECE)
            acc = jnp.dot(u_ref[...], w_ref[:, cs], preferred_element_type=F32)
            o_ref[:, cs] = epilogue(acc).astype(BF16)
        for src, dst in zip(cast_in, cast_out):
            dst[...] = src[...].astype(BF16)

    @pl.when(j < n_v)
    def _():
        project(lambda acc: acc)

    @pl.when(jnp.logical_and(j >= n_v, j < n_v + n_rg))
    def _():
        project(lambda acc: acc * _sigmoid(acc))

    @pl.when(j >= n_v + n_rg)
    def _():
        project(_sigmoid)


def _vg_projection(u, w_in, cast_weights):
    m = u.shape[0]
    tm, tn = PROJ_TM, PROJ_TN
    n_v = D_RET // tn
    n_rg = D_RET // tn
    n_gate = 2 * D_MODEL // tn
    n_j = n_v + n_rg + n_gate
    n_steps = (m // tm) * n_j
    w_v0 = 2 * D_RET // tn
    w_gate_shift = D_POOL // tn
    kern = functools.partial(_vg_kernel, n_v=n_v, n_rg=n_rg, n_cast=len(cast_weights))

    def w_map(i, j):
        return (0, w_v0 + j + jnp.where(j >= n_v + n_rg, w_gate_shift, 0))

    cast_specs = []
    for w in cast_weights:
        rows = -(-w.shape[0] // (n_steps * BF16_SUBLANES)) * BF16_SUBLANES
        while w.shape[0] % rows:
            rows += BF16_SUBLANES
        last = w.shape[0] // rows - 1
        cast_specs.append(pl.BlockSpec(
            (rows, w.shape[1]), lambda i, j, last=last: (jnp.minimum(i * n_j + j, last), 0)))

    outs = pl.pallas_call(
        kern,
        grid=(m // tm, n_j),
        in_specs=[
            pl.BlockSpec((tm, D_MODEL), lambda i, j: (i, 0)),
            pl.BlockSpec((D_MODEL, tn), w_map),
        ] + cast_specs,
        out_specs=[pl.BlockSpec((tm, tn), lambda i, j: (i, j))] + cast_specs,
        out_shape=[jax.ShapeDtypeStruct((m, 2 * D_RET + 2 * D_MODEL), BF16)]
        + [jax.ShapeDtypeStruct(w.shape, BF16) for w in cast_weights],
        compiler_params=_params(),
        name="vg_projection",
    )(u, w_in, *cast_weights)
    return outs[0], outs[1:]


def _pool_kernel(u_ref, w_ref, wpg_ref, pscale_ref, o_ref, halo_ref, p_ref, a_ref, b_ref, *, bps):
    tm = u_ref.shape[0]
    hl = POOL_HALO
    n = tm + hl
    seq_block = pl.program_id(0) % bps
    pos = seq_block * tm + lax.broadcasted_iota(jnp.int32, (tm, 1), 0)
    zeros = jnp.zeros((hl, D_POOL), F32)
    p_ref[0:hl, :] = zeros
    a_ref[0:hl, :] = zeros
    b_ref[0:hl, :] = zeros
    for grp in reversed(range(len(POOL_WINDOWS))):
        win = POOL_WINDOWS[grp]
        cs = slice(grp * POOL_GROUP_DIM, (grp + 1) * POOL_GROUP_DIM)
        if grp % 2 == 1:
            ps = slice((grp - 1) * POOL_GROUP_DIM, (grp + 1) * POOL_GROUP_DIM)
            acc2 = jnp.dot(u_ref[...], w_ref[:, ps], preferred_element_type=F32)
            prev = halo_ref[:, ps]
            p_ref[hl:2 * hl, ps] = jnp.where(seq_block == 0, jnp.zeros_like(prev), prev)
            p_ref[2 * hl:2 * hl + tm, ps] = acc2
            halo_ref[:, ps] = acc2[tm - hl:, :]
        src = p_ref
        for lvl in range(win.bit_length() - 1):
            d = 1 << lvl
            dst = (a_ref, b_ref)[lvl % 2]
            dst[hl:hl + n, cs] = src[hl:hl + n, cs] + src[hl - d:hl - d + n, cs]
            src = dst
        s = src[2 * hl:2 * hl + tm, cs]
        inv_cnt = 1.0 / jnp.minimum(pos + 1, win).astype(F32)
        pooled = s * inv_cnt - p_ref[2 * hl:2 * hl + tm, cs]
        y = jnp.dot(pooled.astype(BF16), wpg_ref[grp], preferred_element_type=F32)
        o_ref[:, cs] = (y * pscale_ref[:, cs]).astype(BF16)


def _pool_projection(u, w_in, wpg, pscale, *, seq):
    m = u.shape[0]
    tm = PROJ_TM
    kern = functools.partial(_pool_kernel, bps=seq // tm)
    return pl.pallas_call(
        kern,
        grid=(m // tm,),
        in_specs=[
            pl.BlockSpec((tm, D_MODEL), lambda i: (i, 0)),
            pl.BlockSpec((D_MODEL, D_POOL), lambda i: (0, 4 * D_RET // D_POOL)),
            pl.BlockSpec((len(POOL_WINDOWS), POOL_GROUP_DIM, POOL_GROUP_DIM), lambda i: (0, 0, 0)),
            pl.BlockSpec((1, D_POOL), lambda i: (0, 0)),
        ],
        out_specs=pl.BlockSpec((tm, D_POOL), lambda i: (i, 0)),
        out_shape=jax.ShapeDtypeStruct((m, D_POOL), BF16),
        scratch_shapes=[pltpu.VMEM((POOL_HALO, D_POOL), F32)]
        + [pltpu.VMEM((tm + 2 * POOL_HALO, D_POOL), F32)] * 3,
        compiler_params=pltpu.CompilerParams(
            dimension_semantics=("arbitrary",), vmem_limit_bytes=VMEM_LIMIT_BYTES),
        name="pool_projection",
    )(u, w_in, wpg, pscale)


def _mixer_kernel(q_ref, k_ref, v_ref, rg_ref, dmask_ref, bdec_ref, pool_ref, gate_ref, x_ref,
                  wr_ref, wp_ref, wo_ref, o_ref, state_ref, og_ref, m_ref):
    @pl.when(pl.program_id(1) == 0)
    def _():
        state_ref[...] = jnp.zeros_like(state_ref)

    for h in range(N_HEADS):
        cs = slice(h * HEAD_DIM, (h + 1) * HEAD_DIM)
        q = q_ref[:, cs]
        k = k_ref[:, cs]
        v = v_ref[:, cs]
        scores = lax.dot_general(q, k, (((1,), (1,)), ((), ())), preferred_element_type=F32)
        p = (scores * dmask_ref[h]).astype(BF16)
        st = state_ref[h]
        o = (jnp.dot(p, v, preferred_element_type=F32)
             + jnp.dot(q, st.astype(BF16), preferred_element_type=F32))
        upd = lax.dot_general(k, v, (((0,), (0,)), ((), ())), preferred_element_type=F32)
        state_ref[h] = st * bdec_ref[h] + upd
        og_ref[:, cs] = (_rms_scale(o) * rg_ref[:, cs].astype(F32)).astype(BF16)

    og = og_ref[...]
    pool = pool_ref[...]
    for c0 in range(0, D_MODEL, MERGE_PIECE):
        cs = slice(c0, c0 + MERGE_PIECE)
        y_ret = jnp.dot(og, wr_ref[:, cs], preferred_element_type=F32)
        y_pool = jnp.dot(pool, wp_ref[:, cs], preferred_element_type=F32)
        g_ret = gate_ref[:, cs].astype(F32)
        g_pool = gate_ref[:, D_MODEL + c0:D_MODEL + c0 + MERGE_PIECE].astype(F32)
        m_ref[:, cs] = (g_ret * y_ret + g_pool * y_pool).astype(BF16)
    merged = m_ref[...]
    for c0 in range(0, D_MODEL, MERGE_PIECE):
        cs = slice(c0, c0 + MERGE_PIECE)
        o_ref[:, cs] = x_ref[:, cs] + jnp.dot(merged, wo_ref[:, cs], preferred_element_type=F32)


def _mixers(qk, vg, pooled, x2d, dmask, bdec, w_ret, w_pool, w_out, *, batch, seq):
    blk = RET_BLOCK
    nblk = seq // blk

    def rows(width, c=0):
        return pl.BlockSpec((blk, width), lambda b, t: (b * nblk + t, c))

    def whole(shape):
        return pl.BlockSpec(shape, lambda b, t: (0,) * len(shape))

    return pl.pallas_call(
        _mixer_kernel,
        grid=(batch, nblk),
        in_specs=[
            rows(D_RET, 0), rows(D_RET, 1),
            rows(D_RET, 0), rows(D_RET, 1),
            whole((N_HEADS, blk, blk)), whole((N_HEADS, 1, HEAD_DIM)),
            rows(D_POOL), rows(2 * D_MODEL, 1),
            rows(D_MODEL),
            whole((D_RET, D_MODEL)), whole((D_POOL, D_MODEL)), whole((D_MODEL, D_MODEL)),
        ],
        out_specs=rows(D_MODEL),
        out_shape=jax.ShapeDtypeStruct((batch * seq, D_MODEL), F32),
        scratch_shapes=[
            pltpu.VMEM((N_HEADS, HEAD_DIM, HEAD_DIM), F32),
            pltpu.VMEM((blk, D_RET), BF16),
            pltpu.VMEM((blk, D_MODEL), BF16),
        ],
        compiler_params=_params(),
        name="mixers",
    )(qk, qk, vg, vg, dmask, bdec, pooled, vg, x2d, w_ret, w_pool, w_out)


def _retention_tables():
    blk = RET_BLOCK
    log_g = jnp.log(1.0 - 2.0 ** (-5.0 - jnp.arange(N_HEADS, dtype=F32)))
    n = jnp.arange(blk, dtype=F32)
    diff = n[:, None] - n[None, :]
    chunk = jnp.arange(blk) // CHUNK
    visible = (chunk[None, :] <= chunk[:, None]).astype(F32)
    dmask = jnp.exp(log_g[:, None, None] * (jnp.abs(diff) - (blk + diff))[None]) * visible[None]
    qdec = jnp.exp(log_g[:, None] * (n[None, :] + 1.0))
    kdec = jnp.exp(log_g[:, None] * (blk - 1.0 - n[None, :]))
    row_decay = jnp.concatenate([qdec, kdec], axis=0)
    row_decay = jnp.broadcast_to(row_decay[:, :, None], (2 * N_HEADS, blk, HEAD_DIM // 2))
    bdec = jnp.exp(log_g * blk)
    bdec = jnp.broadcast_to(bdec[:, None, None], (N_HEADS, 1, HEAD_DIM))
    return dmask, row_decay, bdec


def _ffn_kernel(h_ref, g2_ref, wa_ref, wb_ref, wo_ref, gf_ref, o_ref, u_ref, *, n_ff):
    j = pl.program_id(1)

    @pl.when(j == 0)
    def _():
        _norm_rows_to((u_ref,), h_ref, g2_ref)
        o_ref[...] = h_ref[...]

    u = u_ref[...]
    a = jnp.dot(u, wa_ref[...], preferred_element_type=F32)
    b = jnp.dot(u, wb_ref[...], preferred_element_type=F32)
    gated = (a * _sigmoid(a) * b).astype(BF16)
    o_ref[...] += jnp.dot(gated, wo_ref[...], preferred_element_type=F32)

    @pl.when(j == n_ff - 1)
    def _():
        o_ref[...] = _rms_scale(o_ref[...]) * gf_ref[...]


def _ffn(h1, g2, w_ffn_in, w_ffn_out, g_final, *, tm=FFN_TM, tf=FFN_TF):
    m = h1.shape[0]
    n_ff = D_FF // tf
    kern = functools.partial(_ffn_kernel, n_ff=n_ff)
    return pl.pallas_call(
        kern,
        grid=(m // tm, n_ff),
        in_specs=[
            pl.BlockSpec((tm, D_MODEL), lambda i, j: (i, 0)),
            pl.BlockSpec((1, D_MODEL), lambda i, j: (0, 0)),
            pl.BlockSpec((D_MODEL, tf), lambda i, j: (0, j)),
            pl.BlockSpec((D_MODEL, tf), lambda i, j: (0, n_ff + j)),
            pl.BlockSpec((tf, D_MODEL), lambda i, j: (j, 0)),
            pl.BlockSpec((1, D_MODEL), lambda i, j: (0, 0)),
        ],
        out_specs=pl.BlockSpec((tm, D_MODEL), lambda i, j: (i, 0)),
        out_shape=jax.ShapeDtypeStruct((m, D_MODEL), F32),
        scratch_shapes=[pltpu.VMEM((tm, D_MODEL), BF16)],
        compiler_params=_params(),
        name="swiglu_ffn",
    )(h1, g2, w_ffn_in, w_ffn_in, w_ffn_out, g_final)


def _rotary_tables(seq):
    inv_freq = 1.0 / (ROPE_BASE ** (jnp.arange(0, HEAD_DIM, 2, dtype=F32) / HEAD_DIM))
    ang = jnp.arange(seq).astype(F32)[:, None] * inv_freq[None, :]
    scale = HEAD_DIM ** -0.5
    return jnp.cos(ang) * scale, jnp.sin(ang) * scale


def kernel(x, norm1_g, w_in, w_ret_branch, w_pool_group, pool_scale, w_pool_branch, w_out,
           norm2_g, w_ffn_in, w_ffn_out, norm_final_g):
    batch, seq, d = x.shape
    assert d == D_MODEL and norm1_g.shape[0] == 1 and seq % PROJ_TM == 0
    assert math.log2(HEAD_DIM ** -0.5) == -4.0
    x2d = x.reshape(batch * seq, D_MODEL)
    cos_t, sin_t = _rotary_tables(seq)
    dmask, row_decay, bdec = _retention_tables()
    w_in_b = w_in[0].astype(BF16)

    qk, u = _qk_projection(x2d, norm1_g[0][None, :], w_in_b, cos_t, sin_t, row_decay, seq=seq)
    vg, (w_ret_b, w_pool_b, w_out_b, w_ffn_in_b, w_ffn_out_b) = _vg_projection(
        u, w_in_b, (w_ret_branch[0], w_pool_branch[0], w_out[0], w_ffn_in[0], w_ffn_out[0]))
    pooled = _pool_projection(u, w_in_b, w_pool_group[0].astype(BF16), pool_scale[0][None, :],
                              seq=seq)
    h1 = _mixers(qk, vg, pooled, x2d, dmask, bdec, w_ret_b, w_pool_b, w_out_b,
                 batch=batch, seq=seq)
    out = _ffn(h1, norm2_g[0][None, :], w_ffn_in_b, w_ffn_out_b, norm_final_g[None, :])
    return out.reshape(batch, seq, D_MODEL)
```

```python
import functools
import math

import jax
import jax.numpy as jnp
from jax import lax
from jax.experimental import pallas as pl
from jax.experimental.pallas import tpu as pltpu

D_MODEL = 2048
CHUNK = 64
N_HEADS = 8
HEAD_DIM = 256
D_RET = N_HEADS * HEAD_DIM
POOL_WINDOWS = (2, 4, 8, 16)
POOL_GROUP_DIM = 256
D_POOL = 1024
D_FF = 5632
D_PROJ = 4 * D_RET + D_POOL + 2 * D_MODEL
ROPE_BASE = 10000.0
NORM_EPS = 1e-6
POOL_HALO = 16
assert all(w & (w - 1) == 0 and w <= POOL_HALO for w in POOL_WINDOWS)

VMEM_LIMIT_BYTES = 56 * 1024 * 1024
RET_BLOCK = 256
PROJ_TM = 1024
PROJ_TN = 1024
PROJ_PIECE = 256
NORM_ROWS = 256
FFN_TM = 1024
FFN_TF = 512
MERGE_PIECE = 512

F32 = jnp.float32
BF16 = jnp.bfloat16
BF16_SUBLANES = 16


def _rms_scale(xf):
    return xf * lax.rsqrt(jnp.mean(xf * xf, axis=-1, keepdims=True) + NORM_EPS)


def _sigmoid(x):
    return 0.5 * jnp.tanh(0.5 * x) + 0.5


def _norm_rows_to(dst_refs, src_ref, gain_ref, rows_per_chunk=256):
    n_chunks = src_ref.shape[0] // rows_per_chunk

    def body(c, carry):
        r0 = pl.multiple_of(c * rows_per_chunk, rows_per_chunk)
        xf = src_ref[pl.ds(r0, rows_per_chunk), :]
        u = (_rms_scale(xf) * gain_ref[...]).astype(BF16)
        for dst_ref in dst_refs:
            dst_ref[pl.ds(r0, rows_per_chunk), :] = u
        return carry

    lax.fori_loop(0, n_chunks, body, 0)


def _params():
    return pltpu.CompilerParams(
        dimension_semantics=("arbitrary", "arbitrary"),
        vmem_limit_bytes=VMEM_LIMIT_BYTES)


def _qk_kernel(x_ref, g_ref, w_ref, cos_ref, sin_ref, dec_ref, o_ref, uo_ref, u_ref):
    j = pl.program_id(1)
    half = HEAD_DIM // 2
    tm = o_ref.shape[0]
    heads = range(o_ref.shape[1] // HEAD_DIM)

    def head_rows(hh, r0, acc_rows):
        rs = slice(r0, r0 + RET_BLOCK)
        c0 = hh * HEAD_DIM
        t1 = acc_rows[:, :half]
        t2 = acc_rows[:, half:]
        cos = cos_ref[rs, :]
        sin = sin_ref[rs, :]
        dec = dec_ref[hh]
        o_ref[rs, c0:c0 + half] = ((t1 * cos - t2 * sin) * dec).astype(BF16)
        o_ref[rs, c0 + half:c0 + HEAD_DIM] = ((t1 * sin + t2 * cos) * dec).astype(BF16)

    def head_weights(hh):
        return w_ref[:, hh * HEAD_DIM:(hh + 1) * HEAD_DIM]

    @pl.when(j == 0)
    def _():
        assert RET_BLOCK == NORM_ROWS
        for r0 in range(0, tm, NORM_ROWS):
            rs = slice(r0, r0 + NORM_ROWS)
            u = (_rms_scale(x_ref[rs, :]) * g_ref[...]).astype(BF16)
            u_ref[rs, :] = u
            uo_ref[rs, :] = u
            for hh in heads:
                head_rows(hh, r0, jnp.dot(u, head_weights(hh), preferred_element_type=F32))

    @pl.when(j > 0)
    def _():
        for hh in heads:
            acc = jnp.dot(u_ref[...], head_weights(hh), preferred_element_type=F32)
            for r0 in range(0, tm, RET_BLOCK):
                head_rows(hh, r0, acc[r0:r0 + RET_BLOCK, :])


def _qk_projection(x2d, g1, w_in, cos_t, sin_t, row_decay, *, seq):
    m = x2d.shape[0]
    tm, tn = PROJ_TM, PROJ_TN
    bps = seq // tm
    return pl.pallas_call(
        _qk_kernel,
        grid=(m // tm, 2 * D_RET // tn),
        in_specs=[
            pl.BlockSpec((tm, D_MODEL), lambda i, j: (i, 0)),
            pl.BlockSpec((1, D_MODEL), lambda i, j: (0, 0)),
            pl.BlockSpec((D_MODEL, tn), lambda i, j: (0, j)),
            pl.BlockSpec((tm, HEAD_DIM // 2), lambda i, j: (i % bps, 0)),
            pl.BlockSpec((tm, HEAD_DIM // 2), lambda i, j: (i % bps, 0)),
            pl.BlockSpec((tn // HEAD_DIM, RET_BLOCK, HEAD_DIM // 2), lambda i, j: (j, 0, 0)),
        ],
        out_specs=[
            pl.BlockSpec((tm, tn), lambda i, j: (i, j)),
            pl.BlockSpec((tm, D_MODEL), lambda i, j: (i, 0)),
        ],
        out_shape=[
            jax.ShapeDtypeStruct((m, 2 * D_RET), BF16),
            jax.ShapeDtypeStruct((m, D_MODEL), BF16),
        ],
        scratch_shapes=[pltpu.VMEM((tm, D_MODEL), BF16)],
        compiler_params=_params(),
        name="qk_projection",
    )(x2d, g1, w_in, cos_t, sin_t, row_decay)


def _vg_kernel(u_ref, w_ref, *refs, n_v, n_rg, n_cast):
    j = pl.program_id(1)
    cast_in = refs[:n_cast]
    o_ref = refs[n_cast]
    cast_out = refs[n_cast + 1:]

    def project(epilogue):
        for c0 in range(0, o_ref.shape[1], PROJ_PIECE):
            cs = slice(c0, c0 + PROJ_PIECE)
            acc = jnp.dot(u_ref[...], w_ref[:, cs], preferred_element_type=F32)
            o_ref[:, cs] = epilogue(acc).astype(BF16)
        for src, dst in zip(cast_in, cast_out):
            dst[...] = src[...].astype(BF16)

    @pl.when(j < n_v)
    def _():
        project(lambda acc: acc)

    @pl.when(jnp.logical_and(j >= n_v, j < n_v + n_rg))
    def _():
        project(lambda acc: acc * _sigmoid(acc))

    @pl.when(j >= n_v + n_rg)
    def _():
        project(_sigmoid)


def _vg_projection(u, w_in, cast_weights):
    m = u.shape[0]
    tm, tn = PROJ_TM, PROJ_TN
    n_v = D_RET // tn
    n_rg = D_RET // tn
    n_gate = 2 * D_MODEL // tn
    n_j = n_v + n_rg + n_gate
    n_steps = (m // tm) * n_j
    w_v0 = 2 * D_RET // tn
    w_gate_shift = D_POOL // tn
    kern = functools.partial(_vg_kernel, n_v=n_v, n_rg=n_rg, n_cast=len(cast_weights))

    def w_map(i, j):
        return (0, w_v0 + j + jnp.where(j >= n_v + n_rg, w_gate_shift, 0))

    cast_specs = []
    for w in cast_weights:
        rows = -(-w.shape[0] // (n_steps * BF16_SUBLANES)) * BF16_SUBLANES
        while w.shape[0] % rows:
            rows += BF16_SUBLANES
        last = w.shape[0] // rows - 1
        cast_specs.append(pl.BlockSpec(
            (rows, w.shape[1]), lambda i, j, last=last: (jnp.minimum(i * n_j + j, last), 0)))

    outs = pl.pallas_call(
        kern,
        grid=(m // tm, n_j),
        in_specs=[
            pl.BlockSpec((tm, D_MODEL), lambda i, j: (i, 0)),
            pl.BlockSpec((D_MODEL, tn), w_map),
        ] + cast_specs,
        out_specs=[pl.BlockSpec((tm, tn), lambda i, j: (i, j))] + cast_specs,
        out_shape=[jax.ShapeDtypeStruct((m, 2 * D_RET + 2 * D_MODEL), BF16)]
        + [jax.ShapeDtypeStruct(w.shape, BF16) for w in cast_weights],
        compiler_params=_params(),
        name="vg_projection",
    )(u, w_in, *cast_weights)
    return outs[0], outs[1:]


def _pool_kernel(u_ref, w_ref, wpg_ref, pscale_ref, o_ref, halo_ref, p_ref, a_ref, b_ref, *, bps):
    tm = u_ref.shape[0]
    hl = POOL_HALO
    n = tm + hl
    seq_block = pl.program_id(0) % bps
    pos = seq_block * tm + lax.broadcasted_iota(jnp.int32, (tm, 1), 0)
    zeros = jnp.zeros((hl, D_POOL), F32)
    p_ref[0:hl, :] = zeros
    a_ref[0:hl, :] = zeros
    b_ref[0:hl, :] = zeros
    for grp in reversed(range(len(POOL_WINDOWS))):
        win = POOL_WINDOWS[grp]
        cs = slice(grp * POOL_GROUP_DIM, (grp + 1) * POOL_GROUP_DIM)
        if grp % 2 == 1:
            ps = slice((grp - 1) * POOL_GROUP_DIM, (grp + 1) * POOL_GROUP_DIM)
            acc2 = jnp.dot(u_ref[...], w_ref[:, ps], preferred_element_type=F32)
            prev = halo_ref[:, ps]
            p_ref[hl:2 * hl, ps] = jnp.where(seq_block == 0, jnp.zeros_like(prev), prev)
            p_ref[2 * hl:2 * hl + tm, ps] = acc2
            halo_ref[:, ps] = acc2[tm - hl:, :]
        src = p_ref
        for lvl in range(win.bit_length() - 1):
            d = 1 << lvl
            dst = (a_ref, b_ref)[lvl % 2]
            dst[hl:hl + n, cs] = src[hl:hl + n, cs] + src[hl - d:hl - d + n, cs]
            src = dst
        s = src[2 * hl:2 * hl + tm, cs]
        inv_cnt = 1.0 / jnp.minimum(pos + 1, win).astype(F32)
        pooled = s * inv_cnt - p_ref[2 * hl:2 * hl + tm, cs]
        y = jnp.dot(pooled.astype(BF16), wpg_ref[grp], preferred_element_type=F32)
        o_ref[:, cs] = (y * pscale_ref[:, cs]).astype(BF16)


def _pool_projection(u, w_in, wpg, pscale, *, seq):
    m = u.shape[0]
    tm = PROJ_TM
    kern = functools.partial(_pool_kernel, bps=seq // tm)
    return pl.pallas_call(
        kern,
        grid=(m // tm,),
        in_specs=[
            pl.BlockSpec((tm, D_MODEL), lambda i: (i, 0)),
            pl.BlockSpec((D_MODEL, D_POOL), lambda i: (0, 4 * D_RET // D_POOL)),
            pl.BlockSpec((len(POOL_WINDOWS), POOL_GROUP_DIM, POOL_GROUP_DIM), lambda i: (0, 0, 0)),
            pl.BlockSpec((1, D_POOL), lambda i: (0, 0)),
        ],
        out_specs=pl.BlockSpec((tm, D_POOL), lambda i: (i, 0)),
        out_shape=jax.ShapeDtypeStruct((m, D_POOL), BF16),
        scratch_shapes=[pltpu.VMEM((POOL_HALO, D_POOL), F32)]
        + [pltpu.VMEM((tm + 2 * POOL_HALO, D_POOL), F32)] * 3,
        compiler_params=pltpu.CompilerParams(
            dimension_semantics=("arbitrary",), vmem_limit_bytes=VMEM_LIMIT_BYTES),
        name="pool_projection",
    )(u, w_in, wpg, pscale)


def _mixer_kernel(q_ref, k_ref, v_ref, rg_ref, dmask_ref, bdec_ref, pool_ref, gate_ref, x_ref,
                  wr_ref, wp_ref, wo_ref, o_ref, state_ref, og_ref, m_ref):
    @pl.when(pl.program_id(1) == 0)
    def _():
        state_ref[...] = jnp.zeros_like(state_ref)

    for h in range(N_HEADS):
        cs = slice(h * HEAD_DIM, (h + 1) * HEAD_DIM)
        q = q_ref[:, cs]
        k = k_ref[:, cs]
        v = v_ref[:, cs]
        scores = lax.dot_general(q, k, (((1,), (1,)), ((), ())), preferred_element_type=F32)
        p = (scores * dmask_ref[h]).astype(BF16)
        st = state_ref[h]
        o = (jnp.dot(p, v, preferred_element_type=F32)
             + jnp.dot(q, st.astype(BF16), preferred_element_type=F32))
        upd = lax.dot_general(k, v, (((0,), (0,)), ((), ())), preferred_element_type=F32)
        state_ref[h] = st * bdec_ref[h] + upd
        og_ref[:, cs] = (_rms_scale(o) * rg_ref[:, cs].astype(F32)).astype(BF16)

    og = og_ref[...]
    pool = pool_ref[...]
    for c0 in range(0, D_MODEL, MERGE_PIECE):
        cs = slice(c0, c0 + MERGE_PIECE)
        y_ret = jnp.dot(og, wr_ref[:, cs], preferred_element_type=F32)
        y_pool = jnp.dot(pool, wp_ref[:, cs], preferred_element_type=F32)
        g_ret = gate_ref[:, cs].astype(F32)
        g_pool = gate_ref[:, D_MODEL + c0:D_MODEL + c0 + MERGE_PIECE].astype(F32)
        m_ref[:, cs] = (g_ret * y_ret + g_pool * y_pool).astype(BF16)
    merged = m_ref[...]
    for c0 in range(0, D_MODEL, MERGE_PIECE):
        cs = slice(c0, c0 + MERGE_PIECE)
        o_ref[:, cs] = x_ref[:, cs] + jnp.dot(merged, wo_ref[:, cs], preferred_element_type=F32)


def _mixers(qk, vg, pooled, x2d, dmask, bdec, w_ret, w_pool, w_out, *, batch, seq):
    blk = RET_BLOCK
    nblk = seq // blk

    def rows(width, c=0):
        return pl.BlockSpec((blk, width), lambda b, t: (b * nblk + t, c))

    def whole(shape):
        return pl.BlockSpec(shape, lambda b, t: (0,) * len(shape))

    return pl.pallas_call(
        _mixer_kernel,
        grid=(batch, nblk),
        in_specs=[
            rows(D_RET, 0), rows(D_RET, 1),
            rows(D_RET, 0), rows(D_RET, 1),
            whole((N_HEADS, blk, blk)), whole((N_HEADS, 1, HEAD_DIM)),
            rows(D_POOL), rows(2 * D_MODEL, 1),
            rows(D_MODEL),
            whole((D_RET, D_MODEL)), whole((D_POOL, D_MODEL)), whole((D_MODEL, D_MODEL)),
        ],
        out_specs=rows(D_MODEL),
        out_shape=jax.ShapeDtypeStruct((batch * seq, D_MODEL), F32),
        scratch_shapes=[
            pltpu.VMEM((N_HEADS, HEAD_DIM, HEAD_DIM), F32),
            pltpu.VMEM((blk, D_RET), BF16),
            pltpu.VMEM((blk, D_MODEL), BF16),
        ],
        compiler_params=_params(),
        name="mixers",
    )(qk, qk, vg, vg, dmask, bdec, pooled, vg, x2d, w_ret, w_pool, w_out)


def _retention_tables():
    blk = RET_BLOCK
    log_g = jnp.log(1.0 - 2.0 ** (-5.0 - jnp.arange(N_HEADS, dtype=F32)))
    n = jnp.arange(blk, dtype=F32)
    diff = n[:, None] - n[None, :]
    chunk = jnp.arange(blk) // CHUNK
    visible = (chunk[None, :] <= chunk[:, None]).astype(F32)
    dmask = jnp.exp(log_g[:, None, None] * (jnp.abs(diff) - (blk + diff))[None]) * visible[None]
    qdec = jnp.exp(log_g[:, None] * (n[None, :] + 1.0))
    kdec = jnp.exp(log_g[:, None] * (blk - 1.0 - n[None, :]))
    row_decay = jnp.concatenate([qdec, kdec], axis=0)
    row_decay = jnp.broadcast_to(row_decay[:, :, None], (2 * N_HEADS, blk, HEAD_DIM // 2))
    bdec = jnp.exp(log_g * blk)
    bdec = jnp.broadcast_to(bdec[:, None, None], (N_HEADS, 1, HEAD_DIM))
    return dmask, row_decay, bdec


def _ffn_kernel(h_ref, g2_ref, wa_ref, wb_ref, wo_ref, gf_ref, o_ref, u_ref, *, n_ff):
    j = pl.program_id(1)
    tm = o_ref.shape[0]

    def hidden_block(u, base):
        a = jnp.dot(u, wa_ref[...], preferred_element_type=F32)
        b = jnp.dot(u, wb_ref[...], preferred_element_type=F32)
        gated = (a * _sigmoid(a) * b).astype(BF16)
        return base + jnp.dot(gated, wo_ref[...], preferred_element_type=F32)

    @pl.when(j == 0)
    def _():
        for r0 in range(0, tm, NORM_ROWS):
            rs = slice(r0, r0 + NORM_ROWS)
            h = h_ref[rs, :]
            u = (_rms_scale(h) * g2_ref[...]).astype(BF16)
            u_ref[rs, :] = u
            o_ref[rs, :] = hidden_block(u, h)

    @pl.when(jnp.logical_and(j > 0, j < n_ff - 1))
    def _():
        o_ref[...] = hidden_block(u_ref[...], o_ref[...])

    @pl.when(j == n_ff - 1)
    def _():
        for r0 in range(0, tm, NORM_ROWS):
            rs = slice(r0, r0 + NORM_ROWS)
            h2 = hidden_block(u_ref[rs, :], o_ref[rs, :])
            o_ref[rs, :] = _rms_scale(h2) * gf_ref[...]


def _ffn(h1, g2, w_ffn_in, w_ffn_out, g_final, *, tm=FFN_TM, tf=FFN_TF):
    m = h1.shape[0]
    n_ff = D_FF // tf
    assert n_ff >= 2
    kern = functools.partial(_ffn_kernel, n_ff=n_ff)
    return pl.pallas_call(
        kern,
        grid=(m // tm, n_ff),
        in_specs=[
            pl.BlockSpec((tm, D_MODEL), lambda i, j: (i, 0)),
            pl.BlockSpec((1, D_MODEL), lambda i, j: (0, 0)),
            pl.BlockSpec((D_MODEL, tf), lambda i, j: (0, j)),
            pl.BlockSpec((D_MODEL, tf), lambda i, j: (0, n_ff + j)),
            pl.BlockSpec((tf, D_MODEL), lambda i, j: (j, 0)),
            pl.BlockSpec((1, D_MODEL), lambda i, j: (0, 0)),
        ],
        out_specs=pl.BlockSpec((tm, D_MODEL), lambda i, j: (i, 0)),
        out_shape=jax.ShapeDtypeStruct((m, D_MODEL), F32),
        scratch_shapes=[pltpu.VMEM((tm, D_MODEL), BF16)],
        compiler_params=_params(),
        name="swiglu_ffn",
    )(h1, g2, w_ffn_in, w_ffn_in, w_ffn_out, g_final)


def _rotary_tables(seq):
    inv_freq = 1.0 / (ROPE_BASE ** (jnp.arange(0, HEAD_DIM, 2, dtype=F32) / HEAD_DIM))
    ang = jnp.arange(seq).astype(F32)[:, None] * inv_freq[None, :]
    scale = HEAD_DIM ** -0.5
    return jnp.cos(ang) * scale, jnp.sin(ang) * scale


def kernel(x, norm1_g, w_in, w_ret_branch, w_pool_group, pool_scale, w_pool_branch, w_out,
           norm2_g, w_ffn_in, w_ffn_out, norm_final_g):
    batch, seq, d = x.shape
    assert d == D_MODEL and norm1_g.shape[0] == 1 and seq % PROJ_TM == 0
    assert math.log2(HEAD_DIM ** -0.5) == -4.0
    x2d = x.reshape(batch * seq, D_MODEL)
    cos_t, sin_t = _rotary_tables(seq)
    dmask, row_decay, bdec = _retention_tables()
    w_in_b = w_in[0].astype(BF16)

    qk, u = _qk_projection(x2d, norm1_g[0][None, :], w_in_b, cos_t, sin_t, row_decay, seq=seq)
    vg, (w_ret_b, w_pool_b, w_out_b, w_ffn_in_b, w_ffn_out_b) = _vg_projection(
        u, w_in_b, (w_ret_branch[0], w_pool_branch[0], w_out[0], w_ffn_in[0], w_ffn_out[0]))
    pooled = _pool_projection(u, w_in_b, w_pool_group[0].astype(BF16), pool_scale[0][None, :],
                              seq=seq)
    h1 = _mixers(qk, vg, pooled, x2d, dmask, bdec, w_ret_b, w_pool_b, w_out_b,
                 batch=batch, seq=seq)
    out = _ffn(h1, norm2_g[0][None, :], w_ffn_in_b, w_ffn_out_b, norm_final_g[None, :])
    return out.reshape(batch, seq, D_MODEL)
```

```python
import functools
import math

import jax
import jax.numpy as jnp
from jax import lax
from jax.experimental import pallas as pl
from jax.experimental.pallas import tpu as pltpu

D_MODEL = 2048
CHUNK = 64
N_HEADS = 8
HEAD_DIM = 256
D_RET = N_HEADS * HEAD_DIM
POOL_WINDOWS = (2, 4, 8, 16)
POOL_GROUP_DIM = 256
D_POOL = 1024
D_FF = 5632
D_PROJ = 4 * D_RET + D_POOL + 2 * D_MODEL
ROPE_BASE = 10000.0
NORM_EPS = 1e-6
POOL_HALO = 16
assert all(w & (w - 1) == 0 and w <= POOL_HALO for w in POOL_WINDOWS)

VMEM_LIMIT_BYTES = 56 * 1024 * 1024
RET_BLOCK = 256
PROJ_TM = 1024
PROJ_TN = 1024
PROJ_PIECE = 256
NORM_ROWS = 256
FFN_TM = 1024
FFN_TF = 512
MERGE_PIECE = 512

F32 = jnp.float32
BF16 = jnp.bfloat16
BF16_SUBLANES = 16


def _rms_scale(xf):
    return xf * lax.rsqrt(jnp.mean(xf * xf, axis=-1, keepdims=True) + NORM_EPS)


def _sigmoid(x):
    return 0.5 * jnp.tanh(0.5 * x) + 0.5


def _norm_rows_to(dst_refs, src_ref, gain_ref, rows_per_chunk=256):
    n_chunks = src_ref.shape[0] // rows_per_chunk

    def body(c, carry):
        r0 = pl.multiple_of(c * rows_per_chunk, rows_per_chunk)
        xf = src_ref[pl.ds(r0, rows_per_chunk), :]
        u = (_rms_scale(xf) * gain_ref[...]).astype(BF16)
        for dst_ref in dst_refs:
            dst_ref[pl.ds(r0, rows_per_chunk), :] = u
        return carry

    lax.fori_loop(0, n_chunks, body, 0)


def _params():
    return pltpu.CompilerParams(
        dimension_semantics=("arbitrary", "arbitrary"),
        vmem_limit_bytes=VMEM_LIMIT_BYTES)


def _qk_kernel(x_ref, g_ref, w_ref, cos_ref, sin_ref, dec_ref, wsrc_ref, o_ref, uo_ref, wdst_ref,
               u_ref):
    j = pl.program_id(1)
    half = HEAD_DIM // 2
    tm = o_ref.shape[0]
    heads = range(o_ref.shape[1] // HEAD_DIM)

    def head_rows(hh, r0, acc_rows):
        rs = slice(r0, r0 + RET_BLOCK)
        c0 = hh * HEAD_DIM
        t1 = acc_rows[:, :half]
        t2 = acc_rows[:, half:]
        cos = cos_ref[rs, :]
        sin = sin_ref[rs, :]
        dec = dec_ref[hh]
        o_ref[rs, c0:c0 + half] = ((t1 * cos - t2 * sin) * dec).astype(BF16)
        o_ref[rs, c0 + half:c0 + HEAD_DIM] = ((t1 * sin + t2 * cos) * dec).astype(BF16)

    def head_weights(hh):
        return w_ref[:, hh * HEAD_DIM:(hh + 1) * HEAD_DIM]

    def cast_weight_rows():
        first = wsrc_ref.shape[1] - wdst_ref.shape[0] * PROJ_TN
        for c in range(wdst_ref.shape[0]):
            c0 = first + c * PROJ_TN
            wdst_ref[c] = wsrc_ref[:, c0:c0 + PROJ_TN].astype(BF16)

    @pl.when(j == 0)
    def _():
        assert RET_BLOCK == NORM_ROWS
        for r0 in range(0, tm, NORM_ROWS):
            rs = slice(r0, r0 + NORM_ROWS)
            u = (_rms_scale(x_ref[rs, :]) * g_ref[...]).astype(BF16)
            u_ref[rs, :] = u
            uo_ref[rs, :] = u
            for hh in heads:
                head_rows(hh, r0, jnp.dot(u, head_weights(hh), preferred_element_type=F32))
        cast_weight_rows()

    @pl.when(j > 0)
    def _():
        for hh in heads:
            acc = jnp.dot(u_ref[...], head_weights(hh), preferred_element_type=F32)
            for r0 in range(0, tm, RET_BLOCK):
                head_rows(hh, r0, acc[r0:r0 + RET_BLOCK, :])
        cast_weight_rows()


def _qk_projection(x2d, g1, w_qk, w_in, cos_t, sin_t, row_decay, *, seq):
    m = x2d.shape[0]
    tm, tn = PROJ_TM, PROJ_TN
    bps = seq // tm
    n_j = 2 * D_RET // tn
    n_rest = (D_PROJ - 2 * D_RET) // tn
    cast_rows = D_MODEL // ((m // tm) * n_j)
    assert cast_rows % BF16_SUBLANES == 0
    return pl.pallas_call(
        _qk_kernel,
        grid=(m // tm, 2 * D_RET // tn),
        in_specs=[
            pl.BlockSpec((tm, D_MODEL), lambda i, j: (i, 0)),
            pl.BlockSpec((1, D_MODEL), lambda i, j: (0, 0)),
            pl.BlockSpec((D_MODEL, tn), lambda i, j: (0, j)),
            pl.BlockSpec((tm, HEAD_DIM // 2), lambda i, j: (i % bps, 0)),
            pl.BlockSpec((tm, HEAD_DIM // 2), lambda i, j: (i % bps, 0)),
            pl.BlockSpec((tn // HEAD_DIM, RET_BLOCK, HEAD_DIM // 2), lambda i, j: (j, 0, 0)),
            pl.BlockSpec((cast_rows, D_PROJ), lambda i, j: (i * n_j + j, 0)),
        ],
        out_specs=[
            pl.BlockSpec((tm, tn), lambda i, j: (i, j)),
            pl.BlockSpec((tm, D_MODEL), lambda i, j: (i, 0)),
            pl.BlockSpec((n_rest, cast_rows, tn), lambda i, j: (0, i * n_j + j, 0)),
        ],
        out_shape=[
            jax.ShapeDtypeStruct((m, 2 * D_RET), BF16),
            jax.ShapeDtypeStruct((m, D_MODEL), BF16),
            jax.ShapeDtypeStruct((n_rest, D_MODEL, tn), BF16),
        ],
        scratch_shapes=[pltpu.VMEM((tm, D_MODEL), BF16)],
        compiler_params=_params(),
        name="qk_projection",
    )(x2d, g1, w_qk, cos_t, sin_t, row_decay, w_in)


def _vg_kernel(u_ref, w_ref, *refs, n_v, n_rg, n_cast):
    j = pl.program_id(1)
    cast_in = refs[:n_cast]
    o_ref = refs[n_cast]
    cast_out = refs[n_cast + 1:]

    def project(epilogue):
        for c0 in range(0, o_ref.shape[1], PROJ_PIECE):
            cs = slice(c0, c0 + PROJ_PIECE)
            acc = jnp.dot(u_ref[...], w_ref[:, cs], preferred_element_type=F32)
            o_ref[:, cs] = epilogue(acc).astype(BF16)
        for src, dst in zip(cast_in, cast_out):
            if len(dst.shape) == 2:
                dst[...] = src[...].astype(BF16)
            else:
                width = dst.shape[2]
                for c in range(dst.shape[0]):
                    dst[c] = src[:, c * width:(c + 1) * width].astype(BF16)

    @pl.when(j < n_v)
    def _():
        project(lambda acc: acc)

    @pl.when(jnp.logical_and(j >= n_v, j < n_v + n_rg))
    def _():
        project(lambda acc: acc * _sigmoid(acc))

    @pl.when(j >= n_v + n_rg)
    def _():
        project(_sigmoid)


def _vg_projection(u, w_rest, cast_weights):
    m = u.shape[0]
    tm, tn = PROJ_TM, PROJ_TN
    n_v = D_RET // tn
    n_rg = D_RET // tn
    n_gate = 2 * D_MODEL // tn
    n_j = n_v + n_rg + n_gate
    n_steps = (m // tm) * n_j
    w_gate_shift = D_POOL // tn
    kern = functools.partial(_vg_kernel, n_v=n_v, n_rg=n_rg, n_cast=len(cast_weights))

    def w_map(i, j):
        return (j + jnp.where(j >= n_v + n_rg, w_gate_shift, 0), 0, 0)

    cast_in_specs, cast_out_specs, cast_out_shapes = [], [], []
    for w, width in cast_weights:
        rows = -(-w.shape[0] // (n_steps * BF16_SUBLANES)) * BF16_SUBLANES
        while w.shape[0] % rows:
            rows += BF16_SUBLANES
        last = w.shape[0] // rows - 1

        def row_block(i, j, last=last):
            return jnp.minimum(i * n_j + j, last)

        cast_in_specs.append(pl.BlockSpec((rows, w.shape[1]), lambda i, j, f=row_block: (f(i, j), 0)))
        if width is None:
            cast_out_specs.append(cast_in_specs[-1])
            cast_out_shapes.append(jax.ShapeDtypeStruct(w.shape, BF16))
        else:
            n_blk = w.shape[1] // width
            cast_out_specs.append(pl.BlockSpec(
                (n_blk, rows, width), lambda i, j, f=row_block: (0, f(i, j), 0)))
            cast_out_shapes.append(jax.ShapeDtypeStruct((n_blk, w.shape[0], width), BF16))

    outs = pl.pallas_call(
        kern,
        grid=(m // tm, n_j),
        in_specs=[
            pl.BlockSpec((tm, D_MODEL), lambda i, j: (i, 0)),
            pl.BlockSpec((None, D_MODEL, tn), w_map),
        ] + cast_in_specs,
        out_specs=[pl.BlockSpec((tm, tn), lambda i, j: (i, j))] + cast_out_specs,
        out_shape=[jax.ShapeDtypeStruct((m, 2 * D_RET + 2 * D_MODEL), BF16)] + cast_out_shapes,
        compiler_params=_params(),
        name="vg_projection",
    )(u, w_rest, *[w for w, _ in cast_weights])
    return outs[0], outs[1:]


def _pool_kernel(u_ref, w_ref, wpg_ref, pscale_ref, o_ref, halo_ref, p_ref, a_ref, b_ref, *, bps):
    tm = u_ref.shape[0]
    hl = POOL_HALO
    n = tm + hl
    seq_block = pl.program_id(0) % bps
    pos = seq_block * tm + lax.broadcasted_iota(jnp.int32, (tm, 1), 0)
    zeros = jnp.zeros((hl, D_POOL), F32)
    p_ref[0:hl, :] = zeros
    a_ref[0:hl, :] = zeros
    b_ref[0:hl, :] = zeros
    for grp in reversed(range(len(POOL_WINDOWS))):
        win = POOL_WINDOWS[grp]
        cs = slice(grp * POOL_GROUP_DIM, (grp + 1) * POOL_GROUP_DIM)
        if grp % 2 == 1:
            ps = slice((grp - 1) * POOL_GROUP_DIM, (grp + 1) * POOL_GROUP_DIM)
            acc2 = jnp.dot(u_ref[...], w_ref[:, ps], preferred_element_type=F32)
            prev = halo_ref[:, ps]
            p_ref[hl:2 * hl, ps] = jnp.where(seq_block == 0, jnp.zeros_like(prev), prev)
            p_ref[2 * hl:2 * hl + tm, ps] = acc2
            halo_ref[:, ps] = acc2[tm - hl:, :]
        src = p_ref
        for lvl in range(win.bit_length() - 1):
            d = 1 << lvl
            dst = (a_ref, b_ref)[lvl % 2]
            dst[hl:hl + n, cs] = src[hl:hl + n, cs] + src[hl - d:hl - d + n, cs]
            src = dst
        s = src[2 * hl:2 * hl + tm, cs]
        inv_cnt = 1.0 / jnp.minimum(pos + 1, win).astype(F32)
        pooled = s * inv_cnt - p_ref[2 * hl:2 * hl + tm, cs]
        y = jnp.dot(pooled.astype(BF16), wpg_ref[grp], preferred_element_type=F32)
        o_ref[:, cs] = (y * pscale_ref[:, cs]).astype(BF16)


def _pool_projection(u, w_rest, wpg, pscale, *, seq):
    assert PROJ_TN == D_POOL
    m = u.shape[0]
    tm = PROJ_TM
    kern = functools.partial(_pool_kernel, bps=seq // tm)
    return pl.pallas_call(
        kern,
        grid=(m // tm,),
        in_specs=[
            pl.BlockSpec((tm, D_MODEL), lambda i: (i, 0)),
            pl.BlockSpec((None, D_MODEL, D_POOL), lambda i: (2 * D_RET // D_POOL, 0, 0)),
            pl.BlockSpec((len(POOL_WINDOWS), POOL_GROUP_DIM, POOL_GROUP_DIM), lambda i: (0, 0, 0)),
            pl.BlockSpec((1, D_POOL), lambda i: (0, 0)),
        ],
        out_specs=pl.BlockSpec((tm, D_POOL), lambda i: (i, 0)),
        out_shape=jax.ShapeDtypeStruct((m, D_POOL), BF16),
        scratch_shapes=[pltpu.VMEM((POOL_HALO, D_POOL), F32)]
        + [pltpu.VMEM((tm + 2 * POOL_HALO, D_POOL), F32)] * 3,
        compiler_params=pltpu.CompilerParams(
            dimension_semantics=("arbitrary",), vmem_limit_bytes=VMEM_LIMIT_BYTES),
        name="pool_projection",
    )(u, w_rest, wpg, pscale)


def _mixer_kernel(q_ref, k_ref, v_ref, rg_ref, dmask_ref, bdec_ref, pool_ref, gate_ref, x_ref,
                  wr_ref, wp_ref, wo_ref, o_ref, state_ref, og_ref, m_ref):
    @pl.when(pl.program_id(1) == 0)
    def _():
        state_ref[...] = jnp.zeros_like(state_ref)

    for h in range(N_HEADS):
        cs = slice(h * HEAD_DIM, (h + 1) * HEAD_DIM)
        q = q_ref[:, cs]
        k = k_ref[:, cs]
        v = v_ref[:, cs]
        scores = lax.dot_general(q, k, (((1,), (1,)), ((), ())), preferred_element_type=F32)
        p = (scores * dmask_ref[h]).astype(BF16)
        st = state_ref[h]
        o = (jnp.dot(p, v, preferred_element_type=F32)
             + jnp.dot(q, st.astype(BF16), preferred_element_type=F32))
        upd = lax.dot_general(k, v, (((0,), (0,)), ((), ())), preferred_element_type=F32)
        state_ref[h] = st * bdec_ref[h] + upd
        og_ref[:, cs] = (_rms_scale(o) * rg_ref[:, cs].astype(F32)).astype(BF16)

    og = og_ref[...]
    pool = pool_ref[...]
    for c0 in range(0, D_MODEL, MERGE_PIECE):
        cs = slice(c0, c0 + MERGE_PIECE)
        y_ret = jnp.dot(og, wr_ref[:, cs], preferred_element_type=F32)
        y_pool = jnp.dot(pool, wp_ref[:, cs], preferred_element_type=F32)
        g_ret = gate_ref[:, cs].astype(F32)
        g_pool = gate_ref[:, D_MODEL + c0:D_MODEL + c0 + MERGE_PIECE].astype(F32)
        m_ref[:, cs] = (g_ret * y_ret + g_pool * y_pool).astype(BF16)
    merged = m_ref[...]
    for c0 in range(0, D_MODEL, MERGE_PIECE):
        cs = slice(c0, c0 + MERGE_PIECE)
        o_ref[:, cs] = x_ref[:, cs] + jnp.dot(merged, wo_ref[:, cs], preferred_element_type=F32)


def _mixers(qk, vg, pooled, x2d, dmask, bdec, w_ret, w_pool, w_out, *, batch, seq):
    blk = RET_BLOCK
    nblk = seq // blk

    def rows(width, c=0):
        return pl.BlockSpec((blk, width), lambda b, t: (b * nblk + t, c))

    def whole(shape):
        return pl.BlockSpec(shape, lambda b, t: (0,) * len(shape))

    return pl.pallas_call(
        _mixer_kernel,
        grid=(batch, nblk),
        in_specs=[
            rows(D_RET, 0), rows(D_RET, 1),
            rows(D_RET, 0), rows(D_RET, 1),
            whole((N_HEADS, blk, blk)), whole((N_HEADS, 1, HEAD_DIM)),
            rows(D_POOL), rows(2 * D_MODEL, 1),
            rows(D_MODEL),
            whole((D_RET, D_MODEL)), whole((D_POOL, D_MODEL)), whole((D_MODEL, D_MODEL)),
        ],
        out_specs=rows(D_MODEL),
        out_shape=jax.ShapeDtypeStruct((batch * seq, D_MODEL), F32),
        scratch_shapes=[
            pltpu.VMEM((N_HEADS, HEAD_DIM, HEAD_DIM), F32),
            pltpu.VMEM((blk, D_RET), BF16),
            pltpu.VMEM((blk, D_MODEL), BF16),
        ],
        compiler_params=_params(),
        name="mixers",
    )(qk, qk, vg, vg, dmask, bdec, pooled, vg, x2d, w_ret, w_pool, w_out)


def _retention_tables():
    blk = RET_BLOCK
    log_g = jnp.log(1.0 - 2.0 ** (-5.0 - jnp.arange(N_HEADS, dtype=F32)))
    n = jnp.arange(blk, dtype=F32)
    diff = n[:, None] - n[None, :]
    chunk = jnp.arange(blk) // CHUNK
    visible = (chunk[None, :] <= chunk[:, None]).astype(F32)
    dmask = jnp.exp(log_g[:, None, None] * (jnp.abs(diff) - (blk + diff))[None]) * visible[None]
    qdec = jnp.exp(log_g[:, None] * (n[None, :] + 1.0))
    kdec = jnp.exp(log_g[:, None] * (blk - 1.0 - n[None, :]))
    row_decay = jnp.concatenate([qdec, kdec], axis=0)
    row_decay = jnp.broadcast_to(row_decay[:, :, None], (2 * N_HEADS, blk, HEAD_DIM // 2))
    bdec = jnp.exp(log_g * blk)
    bdec = jnp.broadcast_to(bdec[:, None, None], (N_HEADS, 1, HEAD_DIM))
    return dmask, row_decay, bdec


def _ffn_kernel(h_ref, g2_ref, wa_ref, wb_ref, wo_ref, gf_ref, o_ref, u_ref, *, n_ff):
    j = pl.program_id(1)
    tm = o_ref.shape[0]

    def hidden_block(u, base):
        a = jnp.dot(u, wa_ref[...], preferred_element_type=F32)
        b = jnp.dot(u, wb_ref[...], preferred_element_type=F32)
        gated = (a * _sigmoid(a) * b).astype(BF16)
        return base + jnp.dot(gated, wo_ref[...], preferred_element_type=F32)

    @pl.when(j == 0)
    def _():
        for r0 in range(0, tm, NORM_ROWS):
            rs = slice(r0, r0 + NORM_ROWS)
            h = h_ref[rs, :]
            u = (_rms_scale(h) * g2_ref[...]).astype(BF16)
            u_ref[rs, :] = u
            o_ref[rs, :] = hidden_block(u, h)

    @pl.when(jnp.logical_and(j > 0, j < n_ff - 1))
    def _():
        o_ref[...] = hidden_block(u_ref[...], o_ref[...])

    @pl.when(j == n_ff - 1)
    def _():
        for r0 in range(0, tm, NORM_ROWS):
            rs = slice(r0, r0 + NORM_ROWS)
            h2 = hidden_block(u_ref[rs, :], o_ref[rs, :])
            o_ref[rs, :] = _rms_scale(h2) * gf_ref[...]


def _ffn(h1, g2, w_ffn_in, w_ffn_out, g_final, *, tm=FFN_TM, tf=FFN_TF):
    m = h1.shape[0]
    assert w_ffn_in.shape == (2 * D_FF // tf, D_MODEL, tf)
    n_ff = D_FF // tf
    assert n_ff >= 2
    kern = functools.partial(_ffn_kernel, n_ff=n_ff)
    return pl.pallas_call(
        kern,
        grid=(m // tm, n_ff),
        in_specs=[
            pl.BlockSpec((tm, D_MODEL), lambda i, j: (i, 0)),
            pl.BlockSpec((1, D_MODEL), lambda i, j: (0, 0)),
            pl.BlockSpec((None, D_MODEL, tf), lambda i, j: (j, 0, 0)),
            pl.BlockSpec((None, D_MODEL, tf), lambda i, j: (n_ff + j, 0, 0)),
            pl.BlockSpec((tf, D_MODEL), lambda i, j: (j, 0)),
            pl.BlockSpec((1, D_MODEL), lambda i, j: (0, 0)),
        ],
        out_specs=pl.BlockSpec((tm, D_MODEL), lambda i, j: (i, 0)),
        out_shape=jax.ShapeDtypeStruct((m, D_MODEL), F32),
        scratch_shapes=[pltpu.VMEM((tm, D_MODEL), BF16)],
        compiler_params=_params(),
        name="swiglu_ffn",
    )(h1, g2, w_ffn_in, w_ffn_in, w_ffn_out, g_final)


def _rotary_tables(seq):
    inv_freq = 1.0 / (ROPE_BASE ** (jnp.arange(0, HEAD_DIM, 2, dtype=F32) / HEAD_DIM))
    ang = jnp.arange(seq).astype(F32)[:, None] * inv_freq[None, :]
    scale = HEAD_DIM ** -0.5
    return jnp.cos(ang) * scale, jnp.sin(ang) * scale


def kernel(x, norm1_g, w_in, w_ret_branch, w_pool_group, pool_scale, w_pool_branch, w_out,
           norm2_g, w_ffn_in, w_ffn_out, norm_final_g):
    batch, seq, d = x.shape
    assert d == D_MODEL and norm1_g.shape[0] == 1 and seq % PROJ_TM == 0
    assert math.log2(HEAD_DIM ** -0.5) == -4.0
    x2d = x.reshape(batch * seq, D_MODEL)
    cos_t, sin_t = _rotary_tables(seq)
    dmask, row_decay, bdec = _retention_tables()
    w_qk = w_in[0, :, :2 * D_RET].astype(BF16)

    qk, u, w_rest = _qk_projection(x2d, norm1_g[0][None, :], w_qk, w_in[0], cos_t, sin_t, row_decay,
                                   seq=seq)
    vg, (w_ret_b, w_pool_b, w_out_b, w_ffn_in_b, w_ffn_out_b) = _vg_projection(
        u, w_rest, ((w_ret_branch[0], None), (w_pool_branch[0], None), (w_out[0], None),
                    (w_ffn_in[0], FFN_TF), (w_ffn_out[0], None)))
    pooled = _pool_projection(u, w_rest, w_pool_group[0].astype(BF16), pool_scale[0][None, :],
                              seq=seq)
    h1 = _mixers(qk, vg, pooled, x2d, dmask, bdec, w_ret_b, w_pool_b, w_out_b,
                 batch=batch, seq=seq)
    out = _ffn(h1, norm2_g[0][None, :], w_ffn_in_b, w_ffn_out_b, norm_final_g[None, :])
    return out.reshape(batch, seq, D_MODEL)
```

```python
import functools
import math

import jax
import jax.numpy as jnp
from jax import lax
from jax.experimental import pallas as pl
from jax.experimental.pallas import tpu as pltpu

D_MODEL = 2048
CHUNK = 64
N_HEADS = 8
HEAD_DIM = 256
D_RET = N_HEADS * HEAD_DIM
POOL_WINDOWS = (2, 4, 8, 16)
POOL_GROUP_DIM = 256
D_POOL = 1024
D_FF = 5632
D_PROJ = 4 * D_RET + D_POOL + 2 * D_MODEL
ROPE_BASE = 10000.0
NORM_EPS = 1e-6
POOL_HALO = 16
assert all(w & (w - 1) == 0 and w <= POOL_HALO for w in POOL_WINDOWS)

VMEM_LIMIT_BYTES = 56 * 1024 * 1024
RET_BLOCK = 256
QK_TM = 512
PROJ_TM = 1024
PROJ_TN = 1024
PROJ_PIECE = 256
NORM_ROWS = 256
FFN_TM = 1024
FFN_TF = 512
MERGE_PIECE = 512

F32 = jnp.float32
BF16 = jnp.bfloat16
BF16_SUBLANES = 16


def _rms_scale(xf):
    return xf * lax.rsqrt(jnp.mean(xf * xf, axis=-1, keepdims=True) + NORM_EPS)


def _sigmoid(x):
    return 0.5 * jnp.tanh(0.5 * x) + 0.5


def _norm_rows_to(dst_refs, src_ref, gain_ref, rows_per_chunk=256):
    n_chunks = src_ref.shape[0] // rows_per_chunk

    def body(c, carry):
        r0 = pl.multiple_of(c * rows_per_chunk, rows_per_chunk)
        xf = src_ref[pl.ds(r0, rows_per_chunk), :]
        u = (_rms_scale(xf) * gain_ref[...]).astype(BF16)
        for dst_ref in dst_refs:
            dst_ref[pl.ds(r0, rows_per_chunk), :] = u
        return carry

    lax.fori_loop(0, n_chunks, body, 0)


def _params():
    return pltpu.CompilerParams(
        dimension_semantics=("arbitrary", "arbitrary"),
        vmem_limit_bytes=VMEM_LIMIT_BYTES)


def _qk_kernel(x_ref, g_ref, w_ref, cos_ref, sin_ref, dec_ref, wsrc_ref, o_ref, uo_ref, wdst_ref):
    half = HEAD_DIM // 2
    assert RET_BLOCK == NORM_ROWS
    for r0 in range(0, o_ref.shape[0], NORM_ROWS):
        rs = slice(r0, r0 + NORM_ROWS)
        u = (_rms_scale(x_ref[rs, :]) * g_ref[...]).astype(BF16)
        uo_ref[rs, :] = u
        cos = cos_ref[rs, :]
        sin = sin_ref[rs, :]
        for hh in range(o_ref.shape[1] // HEAD_DIM):
            c0 = hh * HEAD_DIM
            acc = jnp.dot(u, w_ref[:, c0:c0 + HEAD_DIM], preferred_element_type=F32)
            t1 = acc[:, :half]
            t2 = acc[:, half:]
            dec = dec_ref[hh]
            o_ref[rs, c0:c0 + half] = ((t1 * cos - t2 * sin) * dec).astype(BF16)
            o_ref[rs, c0 + half:c0 + HEAD_DIM] = ((t1 * sin + t2 * cos) * dec).astype(BF16)

    first = wsrc_ref.shape[1] - wdst_ref.shape[0] * PROJ_TN
    for c in range(wdst_ref.shape[0]):
        c0 = first + c * PROJ_TN
        wdst_ref[c] = wsrc_ref[:, c0:c0 + PROJ_TN].astype(BF16)


def _qk_projection(x2d, g1, w_qk, w_in, cos_t, sin_t, row_decay, *, seq):
    m = x2d.shape[0]
    tm = QK_TM
    bps = seq // tm
    n_rest = (D_PROJ - 2 * D_RET) // PROJ_TN
    cast_rows = D_MODEL // (m // tm)
    assert cast_rows % BF16_SUBLANES == 0
    return pl.pallas_call(
        _qk_kernel,
        grid=(m // tm,),
        in_specs=[
            pl.BlockSpec((tm, D_MODEL), lambda i: (i, 0)),
            pl.BlockSpec((1, D_MODEL), lambda i: (0, 0)),
            pl.BlockSpec((D_MODEL, 2 * D_RET), lambda i: (0, 0)),
            pl.BlockSpec((tm, HEAD_DIM // 2), lambda i: (i % bps, 0)),
            pl.BlockSpec((tm, HEAD_DIM // 2), lambda i: (i % bps, 0)),
            pl.BlockSpec((2 * N_HEADS, RET_BLOCK, HEAD_DIM // 2), lambda i: (0, 0, 0)),
            pl.BlockSpec((cast_rows, D_PROJ), lambda i: (i, 0)),
        ],
        out_specs=[
            pl.BlockSpec((tm, 2 * D_RET), lambda i: (i, 0)),
            pl.BlockSpec((tm, D_MODEL), lambda i: (i, 0)),
            pl.BlockSpec((n_rest, cast_rows, PROJ_TN), lambda i: (0, i, 0)),
        ],
        out_shape=[
            jax.ShapeDtypeStruct((m, 2 * D_RET), BF16),
            jax.ShapeDtypeStruct((m, D_MODEL), BF16),
            jax.ShapeDtypeStruct((n_rest, D_MODEL, PROJ_TN), BF16),
        ],
        compiler_params=pltpu.CompilerParams(
            dimension_semantics=("arbitrary",), vmem_limit_bytes=VMEM_LIMIT_BYTES),
        name="qk_projection",
    )(x2d, g1, w_qk, cos_t, sin_t, row_decay, w_in)


def _vg_kernel(u_ref, w_ref, *refs, n_v, n_rg, n_cast):
    j = pl.program_id(1)
    cast_in = refs[:n_cast]
    o_ref = refs[n_cast]
    cast_out = refs[n_cast + 1:]

    def project(epilogue):
        for c0 in range(0, o_ref.shape[1], PROJ_PIECE):
            cs = slice(c0, c0 + PROJ_PIECE)
            acc = jnp.dot(u_ref[...], w_ref[:, cs], preferred_element_type=F32)
            o_ref[:, cs] = epilogue(acc).astype(BF16)
        for src, dst in zip(cast_in, cast_out):
            if len(dst.shape) == 2:
                dst[...] = src[...].astype(BF16)
            else:
                width = dst.shape[2]
                for c in range(dst.shape[0]):
                    dst[c] = src[:, c * width:(c + 1) * width].astype(BF16)

    @pl.when(j < n_v)
    def _():
        project(lambda acc: acc)

    @pl.when(jnp.logical_and(j >= n_v, j < n_v + n_rg))
    def _():
        project(lambda acc: acc * _sigmoid(acc))

    @pl.when(j >= n_v + n_rg)
    def _():
        project(_sigmoid)


def _vg_projection(u, w_rest, cast_weights):
    m = u.shape[0]
    tm, tn = PROJ_TM, PROJ_TN
    n_v = D_RET // tn
    n_rg = D_RET // tn
    n_gate = 2 * D_MODEL // tn
    n_j = n_v + n_rg + n_gate
    n_steps = (m // tm) * n_j
    w_gate_shift = D_POOL // tn
    kern = functools.partial(_vg_kernel, n_v=n_v, n_rg=n_rg, n_cast=len(cast_weights))

    def w_map(i, j):
        return (j + jnp.where(j >= n_v + n_rg, w_gate_shift, 0), 0, 0)

    cast_in_specs, cast_out_specs, cast_out_shapes = [], [], []
    for w, width in cast_weights:
        rows = -(-w.shape[0] // (n_steps * BF16_SUBLANES)) * BF16_SUBLANES
        while w.shape[0] % rows:
            rows += BF16_SUBLANES
        last = w.shape[0] // rows - 1

        def row_block(i, j, last=last):
            return jnp.minimum(i * n_j + j, last)

        cast_in_specs.append(pl.BlockSpec((rows, w.shape[1]), lambda i, j, f=row_block: (f(i, j), 0)))
        if width is None:
            cast_out_specs.append(cast_in_specs[-1])
            cast_out_shapes.append(jax.ShapeDtypeStruct(w.shape, BF16))
        else:
            n_blk = w.shape[1] // width
            cast_out_specs.append(pl.BlockSpec(
                (n_blk, rows, width), lambda i, j, f=row_block: (0, f(i, j), 0)))
            cast_out_shapes.append(jax.ShapeDtypeStruct((n_blk, w.shape[0], width), BF16))

    outs = pl.pallas_call(
        kern,
        grid=(m // tm, n_j),
        in_specs=[
            pl.BlockSpec((tm, D_MODEL), lambda i, j: (i, 0)),
            pl.BlockSpec((None, D_MODEL, tn), w_map),
        ] + cast_in_specs,
        out_specs=[pl.BlockSpec((tm, tn), lambda i, j: (i, j))] + cast_out_specs,
        out_shape=[jax.ShapeDtypeStruct((m, 2 * D_RET + 2 * D_MODEL), BF16)] + cast_out_shapes,
        compiler_params=_params(),
        name="vg_projection",
    )(u, w_rest, *[w for w, _ in cast_weights])
    return outs[0], outs[1:]


def _pool_kernel(u_ref, w_ref, wpg_ref, pscale_ref, o_ref, halo_ref, p_ref, a_ref, b_ref, *, bps):
    tm = u_ref.shape[0]
    hl = POOL_HALO
    n = tm + hl
    seq_block = pl.program_id(0) % bps
    pos = seq_block * tm + lax.broadcasted_iota(jnp.int32, (tm, 1), 0)
    zeros = jnp.zeros((hl, D_POOL), F32)
    p_ref[0:hl, :] = zeros
    a_ref[0:hl, :] = zeros
    b_ref[0:hl, :] = zeros
    for grp in reversed(range(len(POOL_WINDOWS))):
        win = POOL_WINDOWS[grp]
        cs = slice(grp * POOL_GROUP_DIM, (grp + 1) * POOL_GROUP_DIM)
        if grp % 2 == 1:
            ps = slice((grp - 1) * POOL_GROUP_DIM, (grp + 1) * POOL_GROUP_DIM)
            acc2 = jnp.dot(u_ref[...], w_ref[:, ps], preferred_element_type=F32)
            prev = halo_ref[:, ps]
            p_ref[hl:2 * hl, ps] = jnp.where(seq_block == 0, jnp.zeros_like(prev), prev)
            p_ref[2 * hl:2 * hl + tm, ps] = acc2
            halo_ref[:, ps] = acc2[tm - hl:, :]
        src = p_ref
        for lvl in range(win.bit_length() - 1):
            d = 1 << lvl
            dst = (a_ref, b_ref)[lvl % 2]
            dst[hl:hl + n, cs] = src[hl:hl + n, cs] + src[hl - d:hl - d + n, cs]
            src = dst
        s = src[2 * hl:2 * hl + tm, cs]
        inv_cnt = 1.0 / jnp.minimum(pos + 1, win).astype(F32)
        pooled = s * inv_cnt - p_ref[2 * hl:2 * hl + tm, cs]
        y = jnp.dot(pooled.astype(BF16), wpg_ref[grp], preferred_element_type=F32)
        o_ref[:, cs] = (y * pscale_ref[:, cs]).astype(BF16)


def _pool_projection(u, w_rest, wpg, pscale, *, seq):
    assert PROJ_TN == D_POOL
    m = u.shape[0]
    tm = PROJ_TM
    kern = functools.partial(_pool_kernel, bps=seq // tm)
    return pl.pallas_call(
        kern,
        grid=(m // tm,),
        in_specs=[
            pl.BlockSpec((tm, D_MODEL), lambda i: (i, 0)),
            pl.BlockSpec((None, D_MODEL, D_POOL), lambda i: (2 * D_RET // D_POOL, 0, 0)),
            pl.BlockSpec((len(POOL_WINDOWS), POOL_GROUP_DIM, POOL_GROUP_DIM), lambda i: (0, 0, 0)),
            pl.BlockSpec((1, D_POOL), lambda i: (0, 0)),
        ],
        out_specs=pl.BlockSpec((tm, D_POOL), lambda i: (i, 0)),
        out_shape=jax.ShapeDtypeStruct((m, D_POOL), BF16),
        scratch_shapes=[pltpu.VMEM((POOL_HALO, D_POOL), F32)]
        + [pltpu.VMEM((tm + 2 * POOL_HALO, D_POOL), F32)] * 3,
        compiler_params=pltpu.CompilerParams(
            dimension_semantics=("arbitrary",), vmem_limit_bytes=VMEM_LIMIT_BYTES),
        name="pool_projection",
    )(u, w_rest, wpg, pscale)


def _mixer_kernel(q_ref, k_ref, v_ref, rg_ref, dmask_ref, bdec_ref, pool_ref, gate_ref, x_ref,
                  wr_ref, wp_ref, wo_ref, o_ref, state_ref, og_ref, m_ref):
    @pl.when(pl.program_id(1) == 0)
    def _():
        state_ref[...] = jnp.zeros_like(state_ref)

    for h in range(N_HEADS):
        cs = slice(h * HEAD_DIM, (h + 1) * HEAD_DIM)
        q = q_ref[:, cs]
        k = k_ref[:, cs]
        v = v_ref[:, cs]
        scores = lax.dot_general(q, k, (((1,), (1,)), ((), ())), preferred_element_type=F32)
        p = (scores * dmask_ref[h]).astype(BF16)
        st = state_ref[h]
        o = (jnp.dot(p, v, preferred_element_type=F32)
             + jnp.dot(q, st.astype(BF16), preferred_element_type=F32))
        upd = lax.dot_general(k, v, (((0,), (0,)), ((), ())), preferred_element_type=F32)
        state_ref[h] = st * bdec_ref[h] + upd
        og_ref[:, cs] = (_rms_scale(o) * rg_ref[:, cs].astype(F32)).astype(BF16)

    og = og_ref[...]
    pool = pool_ref[...]
    for c0 in range(0, D_MODEL, MERGE_PIECE):
        cs = slice(c0, c0 + MERGE_PIECE)
        y_ret = jnp.dot(og, wr_ref[:, cs], preferred_element_type=F32)
        y_pool = jnp.dot(pool, wp_ref[:, cs], preferred_element_type=F32)
        g_ret = gate_ref[:, cs].astype(F32)
        g_pool = gate_ref[:, D_MODEL + c0:D_MODEL + c0 + MERGE_PIECE].astype(F32)
        m_ref[:, cs] = (g_ret * y_ret + g_pool * y_pool).astype(BF16)
    merged = m_ref[...]
    for c0 in range(0, D_MODEL, MERGE_PIECE):
        cs = slice(c0, c0 + MERGE_PIECE)
        o_ref[:, cs] = x_ref[:, cs] + jnp.dot(merged, wo_ref[:, cs], preferred_element_type=F32)


def _mixers(qk, vg, pooled, x2d, dmask, bdec, w_ret, w_pool, w_out, *, batch, seq):
    blk = RET_BLOCK
    nblk = seq // blk

    def rows(width, c=0):
        return pl.BlockSpec((blk, width), lambda b, t: (b * nblk + t, c))

    def whole(shape):
        return pl.BlockSpec(shape, lambda b, t: (0,) * len(shape))

    return pl.pallas_call(
        _mixer_kernel,
        grid=(batch, nblk),
        in_specs=[
            rows(D_RET, 0), rows(D_RET, 1),
            rows(D_RET, 0), rows(D_RET, 1),
            whole((N_HEADS, blk, blk)), whole((N_HEADS, 1, HEAD_DIM)),
            rows(D_POOL), rows(2 * D_MODEL, 1),
            rows(D_MODEL),
            whole((D_RET, D_MODEL)), whole((D_POOL, D_MODEL)), whole((D_MODEL, D_MODEL)),
        ],
        out_specs=rows(D_MODEL),
        out_shape=jax.ShapeDtypeStruct((batch * seq, D_MODEL), F32),
        scratch_shapes=[
            pltpu.VMEM((N_HEADS, HEAD_DIM, HEAD_DIM), F32),
            pltpu.VMEM((blk, D_RET), BF16),
            pltpu.VMEM((blk, D_MODEL), BF16),
        ],
        compiler_params=_params(),
        name="mixers",
    )(qk, qk, vg, vg, dmask, bdec, pooled, vg, x2d, w_ret, w_pool, w_out)


def _retention_tables():
    blk = RET_BLOCK
    log_g = jnp.log(1.0 - 2.0 ** (-5.0 - jnp.arange(N_HEADS, dtype=F32)))
    n = jnp.arange(blk, dtype=F32)
    diff = n[:, None] - n[None, :]
    chunk = jnp.arange(blk) // CHUNK
    visible = (chunk[None, :] <= chunk[:, None]).astype(F32)
    dmask = jnp.exp(log_g[:, None, None] * (jnp.abs(diff) - (blk + diff))[None]) * visible[None]
    qdec = jnp.exp(log_g[:, None] * (n[None, :] + 1.0))
    kdec = jnp.exp(log_g[:, None] * (blk - 1.0 - n[None, :]))
    row_decay = jnp.concatenate([qdec, kdec], axis=0)
    row_decay = jnp.broadcast_to(row_decay[:, :, None], (2 * N_HEADS, blk, HEAD_DIM // 2))
    bdec = jnp.exp(log_g * blk)
    bdec = jnp.broadcast_to(bdec[:, None, None], (N_HEADS, 1, HEAD_DIM))
    return dmask, row_decay, bdec


def _ffn_kernel(h_ref, g2_ref, wa_ref, wb_ref, wo_ref, gf_ref, o_ref, u_ref, *, n_ff):
    j = pl.program_id(1)
    tm = o_ref.shape[0]

    def hidden_block(u, base):
        a = jnp.dot(u, wa_ref[...], preferred_element_type=F32)
        b = jnp.dot(u, wb_ref[...], preferred_element_type=F32)
        gated = (a * _sigmoid(a) * b).astype(BF16)
        return base + jnp.dot(gated, wo_ref[...], preferred_element_type=F32)

    @pl.when(j == 0)
    def _():
        for r0 in range(0, tm, NORM_ROWS):
            rs = slice(r0, r0 + NORM_ROWS)
            h = h_ref[rs, :]
            u = (_rms_scale(h) * g2_ref[...]).astype(BF16)
            u_ref[rs, :] = u
            o_ref[rs, :] = hidden_block(u, h)

    @pl.when(jnp.logical_and(j > 0, j < n_ff - 1))
    def _():
        o_ref[...] = hidden_block(u_ref[...], o_ref[...])

    @pl.when(j == n_ff - 1)
    def _():
        for r0 in range(0, tm, NORM_ROWS):
            rs = slice(r0, r0 + NORM_ROWS)
            h2 = hidden_block(u_ref[rs, :], o_ref[rs, :])
            o_ref[rs, :] = _rms_scale(h2) * gf_ref[...]


def _ffn(h1, g2, w_ffn_in, w_ffn_out, g_final, *, tm=FFN_TM, tf=FFN_TF):
    m = h1.shape[0]
    assert w_ffn_in.shape == (2 * D_FF // tf, D_MODEL, tf)
    n_ff = D_FF // tf
    assert n_ff >= 2
    kern = functools.partial(_ffn_kernel, n_ff=n_ff)
    return pl.pallas_call(
        kern,
        grid=(m // tm, n_ff),
        in_specs=[
            pl.BlockSpec((tm, D_MODEL), lambda i, j: (i, 0)),
            pl.BlockSpec((1, D_MODEL), lambda i, j: (0, 0)),
            pl.BlockSpec((None, D_MODEL, tf), lambda i, j: (j, 0, 0)),
            pl.BlockSpec((None, D_MODEL, tf), lambda i, j: (n_ff + j, 0, 0)),
            pl.BlockSpec((tf, D_MODEL), lambda i, j: (j, 0)),
            pl.BlockSpec((1, D_MODEL), lambda i, j: (0, 0)),
        ],
        out_specs=pl.BlockSpec((tm, D_MODEL), lambda i, j: (i, 0)),
        out_shape=jax.ShapeDtypeStruct((m, D_MODEL), F32),
        scratch_shapes=[pltpu.VMEM((tm, D_MODEL), BF16)],
        compiler_params=_params(),
        name="swiglu_ffn",
    )(h1, g2, w_ffn_in, w_ffn_in, w_ffn_out, g_final)


def _rotary_tables(seq):
    inv_freq = 1.0 / (ROPE_BASE ** (jnp.arange(0, HEAD_DIM, 2, dtype=F32) / HEAD_DIM))
    ang = jnp.arange(seq).astype(F32)[:, None] * inv_freq[None, :]
    scale = HEAD_DIM ** -0.5
    return jnp.cos(ang) * scale, jnp.sin(ang) * scale


def kernel(x, norm1_g, w_in, w_ret_branch, w_pool_group, pool_scale, w_pool_branch, w_out,
           norm2_g, w_ffn_in, w_ffn_out, norm_final_g):
    batch, seq, d = x.shape
    assert d == D_MODEL and norm1_g.shape[0] == 1 and seq % PROJ_TM == 0 and seq % QK_TM == 0
    assert math.log2(HEAD_DIM ** -0.5) == -4.0
    x2d = x.reshape(batch * seq, D_MODEL)
    cos_t, sin_t = _rotary_tables(seq)
    dmask, row_decay, bdec = _retention_tables()
    w_qk = w_in[0, :, :2 * D_RET].astype(BF16)

    qk, u, w_rest = _qk_projection(x2d, norm1_g[0][None, :], w_qk, w_in[0], cos_t, sin_t, row_decay,
                                   seq=seq)
    vg, (w_ret_b, w_pool_b, w_out_b, w_ffn_in_b, w_ffn_out_b) = _vg_projection(
        u, w_rest, ((w_ret_branch[0], None), (w_pool_branch[0], None), (w_out[0], None),
                    (w_ffn_in[0], FFN_TF), (w_ffn_out[0], None)))
    pooled = _pool_projection(u, w_rest, w_pool_group[0].astype(BF16), pool_scale[0][None, :],
                              seq=seq)
    h1 = _mixers(qk, vg, pooled, x2d, dmask, bdec, w_ret_b, w_pool_b, w_out_b,
                 batch=batch, seq=seq)
    out = _ffn(h1, norm2_g[0][None, :], w_ffn_in_b, w_ffn_out_b, norm_final_g[None, :])
    return out.reshape(batch, seq, D_MODEL)
```

```python
import functools
import math

import jax
import jax.numpy as jnp
from jax import lax
from jax.experimental import pallas as pl
from jax.experimental.pallas import tpu as pltpu

D_MODEL = 2048
CHUNK = 64
N_HEADS = 8
HEAD_DIM = 256
D_RET = N_HEADS * HEAD_DIM
POOL_WINDOWS = (2, 4, 8, 16)
POOL_GROUP_DIM = 256
D_POOL = 1024
D_FF = 5632
D_PROJ = 4 * D_RET + D_POOL + 2 * D_MODEL
ROPE_BASE = 10000.0
NORM_EPS = 1e-6
POOL_HALO = 16
assert all(w & (w - 1) == 0 and w <= POOL_HALO for w in POOL_WINDOWS)

VMEM_LIMIT_BYTES = 56 * 1024 * 1024
RET_BLOCK = 256
QK_TM = 512
ACT_TM = 512
PROJ_TM = 1024
PROJ_PIECE = 256
NORM_ROWS = 256
FFN_TM = 1024
FFN_TF = 512
MERGE_PIECE = 512

F32 = jnp.float32
BF16 = jnp.bfloat16
BF16_SUBLANES = 16


def _rms_scale(xf):
    return xf * lax.rsqrt(jnp.mean(xf * xf, axis=-1, keepdims=True) + NORM_EPS)


def _sigmoid(x):
    return 0.5 * jnp.tanh(0.5 * x) + 0.5


def _params():
    return pltpu.CompilerParams(
        dimension_semantics=("arbitrary", "arbitrary"),
        vmem_limit_bytes=VMEM_LIMIT_BYTES)


def _params_1d():
    return pltpu.CompilerParams(
        dimension_semantics=("arbitrary",), vmem_limit_bytes=VMEM_LIMIT_BYTES)


def _qk_kernel(x_ref, g_ref, w_ref, cos_ref, sin_ref, dec_ref, wsrc_ref, o_ref, uo_ref, *wdst_refs):
    half = HEAD_DIM // 2
    assert RET_BLOCK == NORM_ROWS
    for r0 in range(0, o_ref.shape[0], NORM_ROWS):
        rs = slice(r0, r0 + NORM_ROWS)
        u = (_rms_scale(x_ref[rs, :]) * g_ref[...]).astype(BF16)
        uo_ref[rs, :] = u
        cos = cos_ref[rs, :]
        sin = sin_ref[rs, :]
        for hh in range(o_ref.shape[1] // HEAD_DIM):
            c0 = hh * HEAD_DIM
            acc = jnp.dot(u, w_ref[:, c0:c0 + HEAD_DIM], preferred_element_type=F32)
            t1 = acc[:, :half]
            t2 = acc[:, half:]
            dec = dec_ref[hh]
            o_ref[rs, c0:c0 + half] = ((t1 * cos - t2 * sin) * dec).astype(BF16)
            o_ref[rs, c0 + half:c0 + HEAD_DIM] = ((t1 * sin + t2 * cos) * dec).astype(BF16)

    c0 = o_ref.shape[1]
    for wdst_ref in wdst_refs:
        width = wdst_ref.shape[1]
        wdst_ref[...] = wsrc_ref[:, c0:c0 + width].astype(BF16)
        c0 += width


def _qk_projection(x2d, g1, w_qk, w_in, cos_t, sin_t, row_decay, *, seq):
    m = x2d.shape[0]
    tm = QK_TM
    bps = seq // tm
    rest_widths = (2 * D_RET, D_POOL, 2 * D_MODEL)
    assert 2 * D_RET + sum(rest_widths) == D_PROJ
    cast_rows = D_MODEL // (m // tm)
    assert cast_rows % BF16_SUBLANES == 0
    return pl.pallas_call(
        _qk_kernel,
        grid=(m // tm,),
        in_specs=[
            pl.BlockSpec((tm, D_MODEL), lambda i: (i, 0)),
            pl.BlockSpec((1, D_MODEL), lambda i: (0, 0)),
            pl.BlockSpec((D_MODEL, 2 * D_RET), lambda i: (0, 0)),
            pl.BlockSpec((tm, HEAD_DIM // 2), lambda i: (i % bps, 0)),
            pl.BlockSpec((tm, HEAD_DIM // 2), lambda i: (i % bps, 0)),
            pl.BlockSpec((2 * N_HEADS, RET_BLOCK, HEAD_DIM // 2), lambda i: (0, 0, 0)),
            pl.BlockSpec((cast_rows, D_PROJ), lambda i: (i, 0)),
        ],
        out_specs=[
            pl.BlockSpec((tm, 2 * D_RET), lambda i: (i, 0)),
            pl.BlockSpec((tm, D_MODEL), lambda i: (i, 0)),
        ] + [pl.BlockSpec((cast_rows, width), lambda i: (i, 0)) for width in rest_widths],
        out_shape=[
            jax.ShapeDtypeStruct((m, 2 * D_RET), BF16),
            jax.ShapeDtypeStruct((m, D_MODEL), BF16),
        ] + [jax.ShapeDtypeStruct((D_MODEL, width), BF16) for width in rest_widths],
        compiler_params=_params_1d(),
        name="qk_projection",
    )(x2d, g1, w_qk, cos_t, sin_t, row_decay, w_in)


def _act_kernel(u_ref, w_ref, *refs, epilogues, n_cast):
    cast_in = refs[:n_cast]
    o_ref = refs[n_cast]
    cast_out = refs[n_cast + 1:]
    c0 = 0
    for width, epilogue in epilogues:
        for _ in range(width // PROJ_PIECE):
            cs = slice(c0, c0 + PROJ_PIECE)
            acc = jnp.dot(u_ref[...], w_ref[:, cs], preferred_element_type=F32)
            o_ref[:, cs] = epilogue(acc).astype(BF16)
            c0 += PROJ_PIECE
    for src, dst in zip(cast_in, cast_out):
        if len(dst.shape) == 2:
            dst[...] = src[...].astype(BF16)
        else:
            width = dst.shape[2]
            for c in range(dst.shape[0]):
                dst[c] = src[:, c * width:(c + 1) * width].astype(BF16)


def _act_projection(u, w, epilogues, cast_weights, *, name):
    m = u.shape[0]
    tm = ACT_TM
    n_steps = m // tm
    n_out = w.shape[1]
    assert sum(width for width, _ in epilogues) == n_out
    kern = functools.partial(_act_kernel, epilogues=tuple(epilogues), n_cast=len(cast_weights))

    cast_in_specs, cast_out_specs, cast_out_shapes = [], [], []
    for cw, width in cast_weights:
        rows = -(-cw.shape[0] // (n_steps * BF16_SUBLANES)) * BF16_SUBLANES
        while cw.shape[0] % rows:
            rows += BF16_SUBLANES
        last = cw.shape[0] // rows - 1

        def row_block(i, last=last):
            return jnp.minimum(i, last)

        cast_in_specs.append(pl.BlockSpec((rows, cw.shape[1]), lambda i, f=row_block: (f(i), 0)))
        if width is None:
            cast_out_specs.append(cast_in_specs[-1])
            cast_out_shapes.append(jax.ShapeDtypeStruct(cw.shape, BF16))
        else:
            n_blk = cw.shape[1] // width
            cast_out_specs.append(pl.BlockSpec(
                (n_blk, rows, width), lambda i, f=row_block: (0, f(i), 0)))
            cast_out_shapes.append(jax.ShapeDtypeStruct((n_blk, cw.shape[0], width), BF16))

    outs = pl.pallas_call(
        kern,
        grid=(n_steps,),
        in_specs=[
            pl.BlockSpec((tm, D_MODEL), lambda i: (i, 0)),
            pl.BlockSpec((D_MODEL, n_out), lambda i: (0, 0)),
        ] + cast_in_specs,
        out_specs=[pl.BlockSpec((tm, n_out), lambda i: (i, 0))] + cast_out_specs,
        out_shape=[jax.ShapeDtypeStruct((m, n_out), BF16)] + cast_out_shapes,
        compiler_params=_params_1d(),
        name=name,
    )(u, w, *[cw for cw, _ in cast_weights])
    return outs[0], outs[1:]


def _pool_kernel(u_ref, w_ref, wpg_ref, pscale_ref, o_ref, halo_ref, p_ref, a_ref, b_ref, *, bps):
    tm = u_ref.shape[0]
    hl = POOL_HALO
    n = tm + hl
    seq_block = pl.program_id(0) % bps
    pos = seq_block * tm + lax.broadcasted_iota(jnp.int32, (tm, 1), 0)
    zeros = jnp.zeros((hl, D_POOL), F32)
    p_ref[0:hl, :] = zeros
    a_ref[0:hl, :] = zeros
    b_ref[0:hl, :] = zeros
    for grp in reversed(range(len(POOL_WINDOWS))):
        win = POOL_WINDOWS[grp]
        cs = slice(grp * POOL_GROUP_DIM, (grp + 1) * POOL_GROUP_DIM)
        if grp % 2 == 1:
            ps = slice((grp - 1) * POOL_GROUP_DIM, (grp + 1) * POOL_GROUP_DIM)
            acc2 = jnp.dot(u_ref[...], w_ref[:, ps], preferred_element_type=F32)
            prev = halo_ref[:, ps]
            p_ref[hl:2 * hl, ps] = jnp.where(seq_block == 0, jnp.zeros_like(prev), prev)
            p_ref[2 * hl:2 * hl + tm, ps] = acc2
            halo_ref[:, ps] = acc2[tm - hl:, :]
        src = p_ref
        for lvl in range(win.bit_length() - 1):
            d = 1 << lvl
            dst = (a_ref, b_ref)[lvl % 2]
            dst[hl:hl + n, cs] = src[hl:hl + n, cs] + src[hl - d:hl - d + n, cs]
            src = dst
        s = src[2 * hl:2 * hl + tm, cs]
        inv_cnt = 1.0 / jnp.minimum(pos + 1, win).astype(F32)
        pooled = s * inv_cnt - p_ref[2 * hl:2 * hl + tm, cs]
        y = jnp.dot(pooled.astype(BF16), wpg_ref[grp], preferred_element_type=F32)
        o_ref[:, cs] = (y * pscale_ref[:, cs]).astype(BF16)


def _pool_projection(u, w_pool_cols, wpg, pscale, *, seq):
    m = u.shape[0]
    tm = PROJ_TM
    kern = functools.partial(_pool_kernel, bps=seq // tm)
    return pl.pallas_call(
        kern,
        grid=(m // tm,),
        in_specs=[
            pl.BlockSpec((tm, D_MODEL), lambda i: (i, 0)),
            pl.BlockSpec((D_MODEL, D_POOL), lambda i: (0, 0)),
            pl.BlockSpec((len(POOL_WINDOWS), POOL_GROUP_DIM, POOL_GROUP_DIM), lambda i: (0, 0, 0)),
            pl.BlockSpec((1, D_POOL), lambda i: (0, 0)),
        ],
        out_specs=pl.BlockSpec((tm, D_POOL), lambda i: (i, 0)),
        out_shape=jax.ShapeDtypeStruct((m, D_POOL), BF16),
        scratch_shapes=[pltpu.VMEM((POOL_HALO, D_POOL), F32)]
        + [pltpu.VMEM((tm + 2 * POOL_HALO, D_POOL), F32)] * 3,
        compiler_params=_params_1d(),
        name="pool_projection",
    )(u, w_pool_cols, wpg, pscale)


def _mixer_kernel(q_ref, k_ref, v_ref, rg_ref, dmask_ref, bdec_ref, pool_ref, gate_ref, x_ref,
                  wr_ref, wp_ref, wo_ref, o_ref, state_ref, og_ref, m_ref):
    @pl.when(pl.program_id(1) == 0)
    def _():
        state_ref[...] = jnp.zeros_like(state_ref)

    for h in range(N_HEADS):
        cs = slice(h * HEAD_DIM, (h + 1) * HEAD_DIM)
        q = q_ref[:, cs]
        k = k_ref[:, cs]
        v = v_ref[:, cs]
        scores = lax.dot_general(q, k, (((1,), (1,)), ((), ())), preferred_element_type=F32)
        p = (scores * dmask_ref[h]).astype(BF16)
        st = state_ref[h]
        o = (jnp.dot(p, v, preferred_element_type=F32)
             + jnp.dot(q, st.astype(BF16), preferred_element_type=F32))
        upd = lax.dot_general(k, v, (((0,), (0,)), ((), ())), preferred_element_type=F32)
        state_ref[h] = st * bdec_ref[h] + upd
        og_ref[:, cs] = (_rms_scale(o) * rg_ref[:, cs].astype(F32)).astype(BF16)

    og = og_ref[...]
    pool = pool_ref[...]
    for c0 in range(0, D_MODEL, MERGE_PIECE):
        cs = slice(c0, c0 + MERGE_PIECE)
        y_ret = jnp.dot(og, wr_ref[:, cs], preferred_element_type=F32)
        y_pool = jnp.dot(pool, wp_ref[:, cs], preferred_element_type=F32)
        g_ret = gate_ref[:, cs].astype(F32)
        g_pool = gate_ref[:, D_MODEL + c0:D_MODEL + c0 + MERGE_PIECE].astype(F32)
        m_ref[:, cs] = (g_ret * y_ret + g_pool * y_pool).astype(BF16)
    merged = m_ref[...]
    for c0 in range(0, D_MODEL, MERGE_PIECE):
        cs = slice(c0, c0 + MERGE_PIECE)
        o_ref[:, cs] = x_ref[:, cs] + jnp.dot(merged, wo_ref[:, cs], preferred_element_type=F32)


def _mixers(qk, vr, gates, pooled, x2d, dmask, bdec, w_ret, w_pool, w_out, *, batch, seq):
    blk = RET_BLOCK
    nblk = seq // blk

    def rows(width, c=0):
        return pl.BlockSpec((blk, width), lambda b, t: (b * nblk + t, c))

    def whole(shape):
        return pl.BlockSpec(shape, lambda b, t: (0,) * len(shape))

    return pl.pallas_call(
        _mixer_kernel,
        grid=(batch, nblk),
        in_specs=[
            rows(D_RET, 0), rows(D_RET, 1),
            rows(D_RET, 0), rows(D_RET, 1),
            whole((N_HEADS, blk, blk)), whole((N_HEADS, 1, HEAD_DIM)),
            rows(D_POOL), rows(2 * D_MODEL),
            rows(D_MODEL),
            whole((D_RET, D_MODEL)), whole((D_POOL, D_MODEL)), whole((D_MODEL, D_MODEL)),
        ],
        out_specs=rows(D_MODEL),
        out_shape=jax.ShapeDtypeStruct((batch * seq, D_MODEL), F32),
        scratch_shapes=[
            pltpu.VMEM((N_HEADS, HEAD_DIM, HEAD_DIM), F32),
            pltpu.VMEM((blk, D_RET), BF16),
            pltpu.VMEM((blk, D_MODEL), BF16),
        ],
        compiler_params=_params(),
        name="mixers",
    )(qk, qk, vr, vr, dmask, bdec, pooled, gates, x2d, w_ret, w_pool, w_out)


def _retention_tables():
    blk = RET_BLOCK
    log_g = jnp.log(1.0 - 2.0 ** (-5.0 - jnp.arange(N_HEADS, dtype=F32)))
    n = jnp.arange(blk, dtype=F32)
    diff = n[:, None] - n[None, :]
    chunk = jnp.arange(blk) // CHUNK
    visible = (chunk[None, :] <= chunk[:, None]).astype(F32)
    dmask = jnp.exp(log_g[:, None, None] * (jnp.abs(diff) - (blk + diff))[None]) * visible[None]
    qdec = jnp.exp(log_g[:, None] * (n[None, :] + 1.0))
    kdec = jnp.exp(log_g[:, None] * (blk - 1.0 - n[None, :]))
    row_decay = jnp.concatenate([qdec, kdec], axis=0)
    row_decay = jnp.broadcast_to(row_decay[:, :, None], (2 * N_HEADS, blk, HEAD_DIM // 2))
    bdec = jnp.exp(log_g * blk)
    bdec = jnp.broadcast_to(bdec[:, None, None], (N_HEADS, 1, HEAD_DIM))
    return dmask, row_decay, bdec


def _ffn_kernel(h_ref, g2_ref, wa_ref, wb_ref, wo_ref, gf_ref, o_ref, u_ref, *, n_ff):
    j = pl.program_id(1)
    tm = o_ref.shape[0]

    def hidden_block(u, base):
        a = jnp.dot(u, wa_ref[...], preferred_element_type=F32)
        b = jnp.dot(u, wb_ref[...], preferred_element_type=F32)
        gated = (a * _sigmoid(a) * b).astype(BF16)
        return base + jnp.dot(gated, wo_ref[...], preferred_element_type=F32)

    @pl.when(j == 0)
    def _():
        for r0 in range(0, tm, NORM_ROWS):
            rs = slice(r0, r0 + NORM_ROWS)
            h = h_ref[rs, :]
            u = (_rms_scale(h) * g2_ref[...]).astype(BF16)
            u_ref[rs, :] = u
            o_ref[rs, :] = hidden_block(u, h)

    @pl.when(jnp.logical_and(j > 0, j < n_ff - 1))
    def _():
        o_ref[...] = hidden_block(u_ref[...], o_ref[...])

    @pl.when(j == n_ff - 1)
    def _():
        for r0 in range(0, tm, NORM_ROWS):
            rs = slice(r0, r0 + NORM_ROWS)
            h2 = hidden_block(u_ref[rs, :], o_ref[rs, :])
            o_ref[rs, :] = _rms_scale(h2) * gf_ref[...]


def _ffn(h1, g2, w_ffn_in, w_ffn_out, g_final, *, tm=FFN_TM, tf=FFN_TF):
    m = h1.shape[0]
    assert w_ffn_in.shape == (2 * D_FF // tf, D_MODEL, tf)
    n_ff = D_FF // tf
    assert n_ff >= 2
    kern = functools.partial(_ffn_kernel, n_ff=n_ff)
    return pl.pallas_call(
        kern,
        grid=(m // tm, n_ff),
        in_specs=[
            pl.BlockSpec((tm, D_MODEL), lambda i, j: (i, 0)),
            pl.BlockSpec((1, D_MODEL), lambda i, j: (0, 0)),
            pl.BlockSpec((None, D_MODEL, tf), lambda i, j: (j, 0, 0)),
            pl.BlockSpec((None, D_MODEL, tf), lambda i, j: (n_ff + j, 0, 0)),
            pl.BlockSpec((tf, D_MODEL), lambda i, j: (j, 0)),
            pl.BlockSpec((1, D_MODEL), lambda i, j: (0, 0)),
        ],
        out_specs=pl.BlockSpec((tm, D_MODEL), lambda i, j: (i, 0)),
        out_shape=jax.ShapeDtypeStruct((m, D_MODEL), F32),
        scratch_shapes=[pltpu.VMEM((tm, D_MODEL), BF16)],
        compiler_params=_params(),
        name="swiglu_ffn",
    )(h1, g2, w_ffn_in, w_ffn_in, w_ffn_out, g_final)


def _rotary_tables(seq):
    inv_freq = 1.0 / (ROPE_BASE ** (jnp.arange(0, HEAD_DIM, 2, dtype=F32) / HEAD_DIM))
    ang = jnp.arange(seq).astype(F32)[:, None] * inv_freq[None, :]
    scale = HEAD_DIM ** -0.5
    return jnp.cos(ang) * scale, jnp.sin(ang) * scale


def kernel(x, norm1_g, w_in, w_ret_branch, w_pool_group, pool_scale, w_pool_branch, w_out,
           norm2_g, w_ffn_in, w_ffn_out, norm_final_g):
    batch, seq, d = x.shape
    assert d == D_MODEL and norm1_g.shape[0] == 1
    assert seq % PROJ_TM == 0 and seq % QK_TM == 0 and seq % ACT_TM == 0
    assert math.log2(HEAD_DIM ** -0.5) == -4.0
    x2d = x.reshape(batch * seq, D_MODEL)
    cos_t, sin_t = _rotary_tables(seq)
    dmask, row_decay, bdec = _retention_tables()
    w_qk = w_in[0, :, :2 * D_RET].astype(BF16)

    qk, u, w_vr, w_pool_cols, w_gate = _qk_projection(
        x2d, norm1_g[0][None, :], w_qk, w_in[0], cos_t, sin_t, row_decay, seq=seq)
    vr, (w_ret_b, w_pool_b, w_out_b, w_ffn_out_b) = _act_projection(
        u, w_vr, ((D_RET, lambda a: a), (D_RET, lambda a: a * _sigmoid(a))),
        ((w_ret_branch[0], None), (w_pool_branch[0], None), (w_out[0], None), (w_ffn_out[0], None)),
        name="v_rg_projection")
    gates, (w_ffn_in_b,) = _act_projection(
        u, w_gate, ((2 * D_MODEL, _sigmoid),), ((w_ffn_in[0], FFN_TF),), name="gate_projection")
    pooled = _pool_projection(u, w_pool_cols, w_pool_group[0].astype(BF16), pool_scale[0][None, :],
                              seq=seq)
    h1 = _mixers(qk, vr, gates, pooled, x2d, dmask, bdec, w_ret_b, w_pool_b, w_out_b,
                 batch=batch, seq=seq)
    out = _ffn(h1, norm2_g[0][None, :], w_ffn_in_b, w_ffn_out_b, norm_final_g[None, :])
    return out.reshape(batch, seq, D_MODEL)
```

```python
import functools
import math

import jax
import jax.numpy as jnp
import numpy as np
from jax import lax
from jax.experimental import pallas as pl
from jax.experimental.pallas import tpu as pltpu

D_MODEL = 2048
CHUNK = 64
N_HEADS = 8
HEAD_DIM = 256
D_RET = N_HEADS * HEAD_DIM
POOL_WINDOWS = (2, 4, 8, 16)
POOL_GROUP_DIM = 256
D_POOL = 1024
D_FF = 5632
D_PROJ = 4 * D_RET + D_POOL + 2 * D_MODEL
ROPE_BASE = 10000.0
NORM_EPS = 1e-6
POOL_HALO = 16
assert all(w & (w - 1) == 0 and w <= POOL_HALO for w in POOL_WINDOWS)

VMEM_LIMIT_BYTES = 56 * 1024 * 1024
RET_BLOCK = 256
QK_TM = 512
ACT_TM = 512
PROJ_TM = 1024
PROJ_PIECE = 256
NORM_ROWS = 256
FFN_TM = 1024
FFN_TF = 512
MERGE_PIECE = 512

F32 = jnp.float32
BF16 = jnp.bfloat16
BF16_SUBLANES = 16


def _rms_scale(xf):
    return xf * lax.rsqrt(jnp.mean(xf * xf, axis=-1, keepdims=True) + NORM_EPS)


def _sigmoid(x):
    return 0.5 * jnp.tanh(0.5 * x) + 0.5


def _params():
    return pltpu.CompilerParams(
        dimension_semantics=("arbitrary", "arbitrary"),
        vmem_limit_bytes=VMEM_LIMIT_BYTES)


def _params_1d():
    return pltpu.CompilerParams(
        dimension_semantics=("arbitrary",), vmem_limit_bytes=VMEM_LIMIT_BYTES)


def _qk_kernel(x_ref, g_ref, w_ref, cos_ref, sin_ref, dec_ref, wsrc_ref, o_ref, uo_ref, *wdst_refs):
    half = HEAD_DIM // 2
    assert RET_BLOCK == NORM_ROWS
    for r0 in range(0, o_ref.shape[0], NORM_ROWS):
        rs = slice(r0, r0 + NORM_ROWS)
        u = (_rms_scale(x_ref[rs, :]) * g_ref[...]).astype(BF16)
        uo_ref[rs, :] = u
        cos = cos_ref[rs, :]
        sin = sin_ref[rs, :]
        for hh in range(o_ref.shape[1] // HEAD_DIM):
            c0 = hh * HEAD_DIM
            acc = jnp.dot(u, w_ref[:, c0:c0 + HEAD_DIM], preferred_element_type=F32)
            t1 = acc[:, :half]
            t2 = acc[:, half:]
            dec = dec_ref[hh]
            o_ref[rs, c0:c0 + half] = ((t1 * cos - t2 * sin) * dec).astype(BF16)
            o_ref[rs, c0 + half:c0 + HEAD_DIM] = ((t1 * sin + t2 * cos) * dec).astype(BF16)

    c0 = o_ref.shape[1]
    for wdst_ref in wdst_refs:
        width = wdst_ref.shape[1]
        wdst_ref[...] = wsrc_ref[:, c0:c0 + width].astype(BF16)
        c0 += width


def _qk_projection(x2d, g1, w_qk, w_in, cos_t, sin_t, row_decay, *, seq):
    m = x2d.shape[0]
    tm = QK_TM
    bps = seq // tm
    rest_widths = (2 * D_RET, D_POOL, 2 * D_MODEL)
    assert 2 * D_RET + sum(rest_widths) == D_PROJ
    cast_rows = D_MODEL // (m // tm)
    assert cast_rows % BF16_SUBLANES == 0
    return pl.pallas_call(
        _qk_kernel,
        grid=(m // tm,),
        in_specs=[
            pl.BlockSpec((tm, D_MODEL), lambda i: (i, 0)),
            pl.BlockSpec((1, D_MODEL), lambda i: (0, 0)),
            pl.BlockSpec((D_MODEL, 2 * D_RET), lambda i: (0, 0)),
            pl.BlockSpec((tm, HEAD_DIM // 2), lambda i: (i % bps, 0)),
            pl.BlockSpec((tm, HEAD_DIM // 2), lambda i: (i % bps, 0)),
            pl.BlockSpec((2 * N_HEADS, RET_BLOCK, HEAD_DIM // 2), lambda i: (0, 0, 0)),
            pl.BlockSpec((cast_rows, D_PROJ), lambda i: (i, 0)),
        ],
        out_specs=[
            pl.BlockSpec((tm, 2 * D_RET), lambda i: (i, 0)),
            pl.BlockSpec((tm, D_MODEL), lambda i: (i, 0)),
        ] + [pl.BlockSpec((cast_rows, width), lambda i: (i, 0)) for width in rest_widths],
        out_shape=[
            jax.ShapeDtypeStruct((m, 2 * D_RET), BF16),
            jax.ShapeDtypeStruct((m, D_MODEL), BF16),
        ] + [jax.ShapeDtypeStruct((D_MODEL, width), BF16) for width in rest_widths],
        compiler_params=_params_1d(),
        name="qk_projection",
    )(x2d, g1, w_qk, cos_t, sin_t, row_decay, w_in)


def _act_kernel(u_ref, w_ref, *refs, epilogues, n_cast):
    cast_in = refs[:n_cast]
    o_ref = refs[n_cast]
    cast_out = refs[n_cast + 1:]
    c0 = 0
    for width, epilogue in epilogues:
        for _ in range(width // PROJ_PIECE):
            cs = slice(c0, c0 + PROJ_PIECE)
            acc = jnp.dot(u_ref[...], w_ref[:, cs], preferred_element_type=F32)
            o_ref[:, cs] = epilogue(acc).astype(BF16)
            c0 += PROJ_PIECE
    for src, dst in zip(cast_in, cast_out):
        if len(dst.shape) == 2:
            dst[...] = src[...].astype(BF16)
        else:
            width = dst.shape[2]
            for c in range(dst.shape[0]):
                dst[c] = src[:, c * width:(c + 1) * width].astype(BF16)


def _act_projection(u, w, epilogues, cast_weights, *, name):
    m = u.shape[0]
    tm = ACT_TM
    n_steps = m // tm
    n_out = w.shape[1]
    assert sum(width for width, _ in epilogues) == n_out
    kern = functools.partial(_act_kernel, epilogues=tuple(epilogues), n_cast=len(cast_weights))

    cast_in_specs, cast_out_specs, cast_out_shapes = [], [], []
    for cw, width in cast_weights:
        rows = -(-cw.shape[0] // (n_steps * BF16_SUBLANES)) * BF16_SUBLANES
        while cw.shape[0] % rows:
            rows += BF16_SUBLANES
        last = cw.shape[0] // rows - 1

        def row_block(i, last=last):
            return jnp.minimum(i, last)

        cast_in_specs.append(pl.BlockSpec((rows, cw.shape[1]), lambda i, f=row_block: (f(i), 0)))
        if width is None:
            cast_out_specs.append(cast_in_specs[-1])
            cast_out_shapes.append(jax.ShapeDtypeStruct(cw.shape, BF16))
        else:
            n_blk = cw.shape[1] // width
            cast_out_specs.append(pl.BlockSpec(
                (n_blk, rows, width), lambda i, f=row_block: (0, f(i), 0)))
            cast_out_shapes.append(jax.ShapeDtypeStruct((n_blk, cw.shape[0], width), BF16))

    outs = pl.pallas_call(
        kern,
        grid=(n_steps,),
        in_specs=[
            pl.BlockSpec((tm, D_MODEL), lambda i: (i, 0)),
            pl.BlockSpec((D_MODEL, n_out), lambda i: (0, 0)),
        ] + cast_in_specs,
        out_specs=[pl.BlockSpec((tm, n_out), lambda i: (i, 0))] + cast_out_specs,
        out_shape=[jax.ShapeDtypeStruct((m, n_out), BF16)] + cast_out_shapes,
        compiler_params=_params_1d(),
        name=name,
    )(u, w, *[cw for cw, _ in cast_weights])
    return outs[0], outs[1:]


def _pool_kernel(u_ref, w_ref, wpg_ref, pscale_ref, o_ref, halo_ref, p_ref, a_ref, b_ref, *, bps):
    tm = u_ref.shape[0]
    hl = POOL_HALO
    n = tm + hl
    seq_block = pl.program_id(0) % bps
    pos = seq_block * tm + lax.broadcasted_iota(jnp.int32, (tm, 1), 0)
    zeros = jnp.zeros((hl, D_POOL), F32)
    p_ref[0:hl, :] = zeros
    a_ref[0:hl, :] = zeros
    b_ref[0:hl, :] = zeros
    for grp in reversed(range(len(POOL_WINDOWS))):
        win = POOL_WINDOWS[grp]
        cs = slice(grp * POOL_GROUP_DIM, (grp + 1) * POOL_GROUP_DIM)
        if grp % 2 == 1:
            ps = slice((grp - 1) * POOL_GROUP_DIM, (grp + 1) * POOL_GROUP_DIM)
            acc2 = jnp.dot(u_ref[...], w_ref[:, ps], preferred_element_type=F32)
            prev = halo_ref[:, ps]
            p_ref[hl:2 * hl, ps] = jnp.where(seq_block == 0, jnp.zeros_like(prev), prev)
            p_ref[2 * hl:2 * hl + tm, ps] = acc2
            halo_ref[:, ps] = acc2[tm - hl:, :]
        src = p_ref
        for lvl in range(win.bit_length() - 1):
            d = 1 << lvl
            dst = (a_ref, b_ref)[lvl % 2]
            dst[hl:hl + n, cs] = src[hl:hl + n, cs] + src[hl - d:hl - d + n, cs]
            src = dst
        s = src[2 * hl:2 * hl + tm, cs]
        inv_cnt = 1.0 / jnp.minimum(pos + 1, win).astype(F32)
        pooled = s * inv_cnt - p_ref[2 * hl:2 * hl + tm, cs]
        y = jnp.dot(pooled.astype(BF16), wpg_ref[grp], preferred_element_type=F32)
        o_ref[:, cs] = (y * pscale_ref[:, cs]).astype(BF16)


def _pool_projection(u, w_pool_cols, wpg, pscale, *, seq):
    m = u.shape[0]
    tm = PROJ_TM
    kern = functools.partial(_pool_kernel, bps=seq // tm)
    return pl.pallas_call(
        kern,
        grid=(m // tm,),
        in_specs=[
            pl.BlockSpec((tm, D_MODEL), lambda i: (i, 0)),
            pl.BlockSpec((D_MODEL, D_POOL), lambda i: (0, 0)),
            pl.BlockSpec((len(POOL_WINDOWS), POOL_GROUP_DIM, POOL_GROUP_DIM), lambda i: (0, 0, 0)),
            pl.BlockSpec((1, D_POOL), lambda i: (0, 0)),
        ],
        out_specs=pl.BlockSpec((tm, D_POOL), lambda i: (i, 0)),
        out_shape=jax.ShapeDtypeStruct((m, D_POOL), BF16),
        scratch_shapes=[pltpu.VMEM((POOL_HALO, D_POOL), F32)]
        + [pltpu.VMEM((tm + 2 * POOL_HALO, D_POOL), F32)] * 3,
        compiler_params=_params_1d(),
        name="pool_projection",
    )(u, w_pool_cols, wpg, pscale)


def _mixer_kernel(q_ref, k_ref, v_ref, rg_ref, dmask_ref, bdec_ref, pool_ref, gate_ref, x_ref,
                  wr_ref, wp_ref, wo_ref, o_ref, state_ref, og_ref, m_ref):
    @pl.when(pl.program_id(1) == 0)
    def _():
        state_ref[...] = jnp.zeros_like(state_ref)

    for h in range(N_HEADS):
        cs = slice(h * HEAD_DIM, (h + 1) * HEAD_DIM)
        q = q_ref[:, cs]
        k = k_ref[:, cs]
        v = v_ref[:, cs]
        scores = lax.dot_general(q, k, (((1,), (1,)), ((), ())), preferred_element_type=F32)
        p = (scores * dmask_ref[h]).astype(BF16)
        st = state_ref[h]
        o = (jnp.dot(p, v, preferred_element_type=F32)
             + jnp.dot(q, st.astype(BF16), preferred_element_type=F32))
        upd = lax.dot_general(k, v, (((0,), (0,)), ((), ())), preferred_element_type=F32)
        state_ref[h] = st * bdec_ref[h] + upd
        og_ref[:, cs] = (_rms_scale(o) * rg_ref[:, cs].astype(F32)).astype(BF16)

    og = og_ref[...]
    pool = pool_ref[...]
    for c0 in range(0, D_MODEL, MERGE_PIECE):
        cs = slice(c0, c0 + MERGE_PIECE)
        y_ret = jnp.dot(og, wr_ref[:, cs], preferred_element_type=F32)
        y_pool = jnp.dot(pool, wp_ref[:, cs], preferred_element_type=F32)
        g_ret = gate_ref[:, cs].astype(F32)
        g_pool = gate_ref[:, D_MODEL + c0:D_MODEL + c0 + MERGE_PIECE].astype(F32)
        m_ref[:, cs] = (g_ret * y_ret + g_pool * y_pool).astype(BF16)
    merged = m_ref[...]
    for c0 in range(0, D_MODEL, MERGE_PIECE):
        cs = slice(c0, c0 + MERGE_PIECE)
        o_ref[:, cs] = x_ref[:, cs] + jnp.dot(merged, wo_ref[:, cs], preferred_element_type=F32)


def _mixers(qk, vr, gates, pooled, x2d, dmask, bdec, w_ret, w_pool, w_out, *, batch, seq):
    blk = RET_BLOCK
    nblk = seq // blk

    def rows(width, c=0):
        return pl.BlockSpec((blk, width), lambda b, t: (b * nblk + t, c))

    def whole(shape):
        return pl.BlockSpec(shape, lambda b, t: (0,) * len(shape))

    return pl.pallas_call(
        _mixer_kernel,
        grid=(batch, nblk),
        in_specs=[
            rows(D_RET, 0), rows(D_RET, 1),
            rows(D_RET, 0), rows(D_RET, 1),
            whole((N_HEADS, blk, blk)), whole((N_HEADS, 1, HEAD_DIM)),
            rows(D_POOL), rows(2 * D_MODEL),
            rows(D_MODEL),
            whole((D_RET, D_MODEL)), whole((D_POOL, D_MODEL)), whole((D_MODEL, D_MODEL)),
        ],
        out_specs=rows(D_MODEL),
        out_shape=jax.ShapeDtypeStruct((batch * seq, D_MODEL), F32),
        scratch_shapes=[
            pltpu.VMEM((N_HEADS, HEAD_DIM, HEAD_DIM), F32),
            pltpu.VMEM((blk, D_RET), BF16),
            pltpu.VMEM((blk, D_MODEL), BF16),
        ],
        compiler_params=_params(),
        name="mixers",
    )(qk, qk, vr, vr, dmask, bdec, pooled, gates, x2d, w_ret, w_pool, w_out)


def _retention_tables():
    blk = RET_BLOCK
    f32 = np.float32
    log_g = np.log(f32(1.0) - f32(2.0) ** (f32(-5.0) - np.arange(N_HEADS, dtype=f32)))
    n = np.arange(blk, dtype=f32)
    diff = n[:, None] - n[None, :]
    chunk = np.arange(blk) // CHUNK
    visible = (chunk[None, :] <= chunk[:, None]).astype(f32)
    dmask = np.exp(log_g[:, None, None] * (np.abs(diff) - (f32(blk) + diff))[None]) * visible[None]
    qdec = np.exp(log_g[:, None] * (n[None, :] + f32(1.0)))
    kdec = np.exp(log_g[:, None] * (f32(blk - 1.0) - n[None, :]))
    row_decay = np.concatenate([qdec, kdec], axis=0)
    row_decay = np.broadcast_to(row_decay[:, :, None], (2 * N_HEADS, blk, HEAD_DIM // 2))
    bdec = np.exp(log_g * f32(blk))
    bdec = np.broadcast_to(bdec[:, None, None], (N_HEADS, 1, HEAD_DIM))
    return dmask, row_decay, bdec


def _ffn_kernel(*refs, n_ff, n_chunks):
    h_refs = refs[:n_chunks]
    g2_ref, wa_ref, wb_ref, wo_ref, gf_ref, o_ref, u_ref = refs[n_chunks:]
    j = pl.program_id(1)
    tm = o_ref.shape[0]

    def hidden_block(u, base):
        a = jnp.dot(u, wa_ref[...], preferred_element_type=F32)
        b = jnp.dot(u, wb_ref[...], preferred_element_type=F32)
        gated = (a * _sigmoid(a) * b).astype(BF16)
        return base + jnp.dot(gated, wo_ref[...], preferred_element_type=F32)

    @pl.when(j == 0)
    def _():
        for c, r0 in enumerate(range(0, tm, NORM_ROWS)):
            rs = slice(r0, r0 + NORM_ROWS)
            h = h_refs[c][...]
            u = (_rms_scale(h) * g2_ref[...]).astype(BF16)
            u_ref[rs, :] = u
            o_ref[rs, :] = hidden_block(u, h)

    @pl.when(jnp.logical_and(j > 0, j < n_ff - 1))
    def _():
        o_ref[...] = hidden_block(u_ref[...], o_ref[...])

    @pl.when(j == n_ff - 1)
    def _():
        for r0 in range(0, tm, NORM_ROWS):
            rs = slice(r0, r0 + NORM_ROWS)
            h2 = hidden_block(u_ref[rs, :], o_ref[rs, :])
            o_ref[rs, :] = _rms_scale(h2) * gf_ref[...]


def _ffn(h1, g2, w_ffn_in, w_ffn_out, g_final, *, tm=FFN_TM, tf=FFN_TF):
    m = h1.shape[0]
    assert w_ffn_in.shape == (2 * D_FF // tf, D_MODEL, tf)
    n_ff = D_FF // tf
    n_chunks = tm // NORM_ROWS
    assert n_ff >= max(2, n_chunks)
    kern = functools.partial(_ffn_kernel, n_ff=n_ff, n_chunks=n_chunks)
    last_chunk_row = m // NORM_ROWS - n_chunks

    def h_chunk_spec(c):
        def index(i, j):
            nxt = (j >= n_ff - n_chunks + c).astype(jnp.int32)
            return (jnp.minimum(n_chunks * (i + nxt), last_chunk_row) + c, 0)
        return pl.BlockSpec((NORM_ROWS, D_MODEL), index)

    return pl.pallas_call(
        kern,
        grid=(m // tm, n_ff),
        in_specs=[h_chunk_spec(c) for c in range(n_chunks)] + [
            pl.BlockSpec((1, D_MODEL), lambda i, j: (0, 0)),
            pl.BlockSpec((None, D_MODEL, tf), lambda i, j: (j, 0, 0)),
            pl.BlockSpec((None, D_MODEL, tf), lambda i, j: (n_ff + j, 0, 0)),
            pl.BlockSpec((tf, D_MODEL), lambda i, j: (j, 0)),
            pl.BlockSpec((1, D_MODEL), lambda i, j: (0, 0)),
        ],
        out_specs=pl.BlockSpec((tm, D_MODEL), lambda i, j: (i, 0)),
        out_shape=jax.ShapeDtypeStruct((m, D_MODEL), F32),
        scratch_shapes=[pltpu.VMEM((tm, D_MODEL), BF16)],
        compiler_params=_params(),
        name="swiglu_ffn",
    )(*([h1] * n_chunks), g2, w_ffn_in, w_ffn_in, w_ffn_out, g_final)


def _rotary_tables(seq):
    f32 = np.float32
    inv_freq = f32(1.0) / (f32(ROPE_BASE) ** (np.arange(0, HEAD_DIM, 2, dtype=f32) / f32(HEAD_DIM)))
    ang = np.arange(seq).astype(f32)[:, None] * inv_freq[None, :]
    scale = f32(HEAD_DIM ** -0.5)
    return np.cos(ang) * scale, np.sin(ang) * scale


def kernel(x, norm1_g, w_in, w_ret_branch, w_pool_group, pool_scale, w_pool_branch, w_out,
           norm2_g, w_ffn_in, w_ffn_out, norm_final_g):
    batch, seq, d = x.shape
    assert d == D_MODEL and norm1_g.shape[0] == 1
    assert seq % PROJ_TM == 0 and seq % QK_TM == 0 and seq % ACT_TM == 0
    assert math.log2(HEAD_DIM ** -0.5) == -4.0
    x2d = x.reshape(batch * seq, D_MODEL)
    cos_t, sin_t = _rotary_tables(seq)
    dmask, row_decay, bdec = _retention_tables()
    w_qk = w_in[0, :, :2 * D_RET].astype(BF16)

    qk, u, w_vr, w_pool_cols, w_gate = _qk_projection(
        x2d, norm1_g[0][None, :], w_qk, w_in[0], cos_t, sin_t, row_decay, seq=seq)
    vr, (w_ret_b, w_pool_b, w_out_b, w_ffn_out_b) = _act_projection(
        u, w_vr, ((D_RET, lambda a: a), (D_RET, lambda a: a * _sigmoid(a))),
        ((w_ret_branch[0], None), (w_pool_branch[0], None), (w_out[0], None), (w_ffn_out[0], None)),
        name="v_rg_projection")
    gates, (w_ffn_in_b,) = _act_projection(
        u, w_gate, ((2 * D_MODEL, _sigmoid),), ((w_ffn_in[0], FFN_TF),), name="gate_projection")
    pooled = _pool_projection(u, w_pool_cols, w_pool_group[0].astype(BF16), pool_scale[0][None, :],
                              seq=seq)
    h1 = _mixers(qk, vr, gates, pooled, x2d, dmask, bdec, w_ret_b, w_pool_b, w_out_b,
                 batch=batch, seq=seq)
    out = _ffn(h1, norm2_g[0][None, :], w_ffn_in_b, w_ffn_out_b, norm_final_g[None, :])
    return out.reshape(batch, seq, D_MODEL)
```

```python
import functools
import math

import jax
import jax.numpy as jnp
import numpy as np
from jax import lax
from jax.experimental import pallas as pl
from jax.experimental.pallas import tpu as pltpu

D_MODEL = 2048
CHUNK = 64
N_HEADS = 8
HEAD_DIM = 256
D_RET = N_HEADS * HEAD_DIM
POOL_WINDOWS = (2, 4, 8, 16)
POOL_GROUP_DIM = 256
D_POOL = 1024
D_FF = 5632
D_PROJ = 4 * D_RET + D_POOL + 2 * D_MODEL
ROPE_BASE = 10000.0
NORM_EPS = 1e-6
POOL_HALO = 16
assert all(w & (w - 1) == 0 and w <= POOL_HALO for w in POOL_WINDOWS)

VMEM_LIMIT_BYTES = 56 * 1024 * 1024
RET_BLOCK = 256
QK_TM = 512
ACT_TM = 512
PROJ_TM = 1024
PROJ_PIECE = 256
NORM_ROWS = 256
FFN_TM = 1024
FFN_TF = 512
FFN_NORM_ROWS = 512
MERGE_PIECE = 512

F32 = jnp.float32
BF16 = jnp.bfloat16
BF16_SUBLANES = 16


def _rms_scale(xf):
    return xf * lax.rsqrt(jnp.mean(xf * xf, axis=-1, keepdims=True) + NORM_EPS)


def _sigmoid(x):
    return 0.5 * jnp.tanh(0.5 * x) + 0.5


def _params():
    return pltpu.CompilerParams(
        dimension_semantics=("arbitrary", "arbitrary"),
        vmem_limit_bytes=VMEM_LIMIT_BYTES)


def _params_1d():
    return pltpu.CompilerParams(
        dimension_semantics=("arbitrary",), vmem_limit_bytes=VMEM_LIMIT_BYTES)


def _qk_kernel(x_ref, g_ref, w_ref, cos_ref, sin_ref, dec_ref, wsrc_ref, o_ref, uo_ref, *wdst_refs):
    half = HEAD_DIM // 2
    assert RET_BLOCK == NORM_ROWS
    for r0 in range(0, o_ref.shape[0], NORM_ROWS):
        rs = slice(r0, r0 + NORM_ROWS)
        u = (_rms_scale(x_ref[rs, :]) * g_ref[...]).astype(BF16)
        uo_ref[rs, :] = u
        cos = cos_ref[rs, :]
        sin = sin_ref[rs, :]
        for hh in range(o_ref.shape[1] // HEAD_DIM):
            c0 = hh * HEAD_DIM
            acc = jnp.dot(u, w_ref[:, c0:c0 + HEAD_DIM], preferred_element_type=F32)
            t1 = acc[:, :half]
            t2 = acc[:, half:]
            dec = dec_ref[hh]
            o_ref[rs, c0:c0 + half] = ((t1 * cos - t2 * sin) * dec).astype(BF16)
            o_ref[rs, c0 + half:c0 + HEAD_DIM] = ((t1 * sin + t2 * cos) * dec).astype(BF16)

    c0 = o_ref.shape[1]
    for wdst_ref in wdst_refs:
        width = wdst_ref.shape[1]
        wdst_ref[...] = wsrc_ref[:, c0:c0 + width].astype(BF16)
        c0 += width


def _qk_projection(x2d, g1, w_qk, w_in, cos_t, sin_t, row_decay, *, seq):
    m = x2d.shape[0]
    tm = QK_TM
    bps = seq // tm
    rest_widths = (2 * D_RET, D_POOL, 2 * D_MODEL)
    assert 2 * D_RET + sum(rest_widths) == D_PROJ
    cast_rows = D_MODEL // (m // tm)
    assert cast_rows % BF16_SUBLANES == 0
    return pl.pallas_call(
        _qk_kernel,
        grid=(m // tm,),
        in_specs=[
            pl.BlockSpec((tm, D_MODEL), lambda i: (i, 0)),
            pl.BlockSpec((1, D_MODEL), lambda i: (0, 0)),
            pl.BlockSpec((D_MODEL, 2 * D_RET), lambda i: (0, 0)),
            pl.BlockSpec((tm, HEAD_DIM // 2), lambda i: (i % bps, 0)),
            pl.BlockSpec((tm, HEAD_DIM // 2), lambda i: (i % bps, 0)),
            pl.BlockSpec((2 * N_HEADS, RET_BLOCK, HEAD_DIM // 2), lambda i: (0, 0, 0)),
            pl.BlockSpec((cast_rows, D_PROJ), lambda i: (i, 0)),
        ],
        out_specs=[
            pl.BlockSpec((tm, 2 * D_RET), lambda i: (i, 0)),
            pl.BlockSpec((tm, D_MODEL), lambda i: (i, 0)),
        ] + [pl.BlockSpec((cast_rows, width), lambda i: (i, 0)) for width in rest_widths],
        out_shape=[
            jax.ShapeDtypeStruct((m, 2 * D_RET), BF16),
            jax.ShapeDtypeStruct((m, D_MODEL), BF16),
        ] + [jax.ShapeDtypeStruct((D_MODEL, width), BF16) for width in rest_widths],
        compiler_params=_params_1d(),
        name="qk_projection",
    )(x2d, g1, w_qk, cos_t, sin_t, row_decay, w_in)


def _act_kernel(u_ref, w_ref, *refs, epilogues, n_cast):
    cast_in = refs[:n_cast]
    o_ref = refs[n_cast]
    cast_out = refs[n_cast + 1:]
    c0 = 0
    for width, epilogue in epilogues:
        for _ in range(width // PROJ_PIECE):
            cs = slice(c0, c0 + PROJ_PIECE)
            acc = jnp.dot(u_ref[...], w_ref[:, cs], preferred_element_type=F32)
            o_ref[:, cs] = epilogue(acc).astype(BF16)
            c0 += PROJ_PIECE
    for src, dst in zip(cast_in, cast_out):
        if len(dst.shape) == 2:
            dst[...] = src[...].astype(BF16)
        else:
            width = dst.shape[2]
            for c in range(dst.shape[0]):
                dst[c] = src[:, c * width:(c + 1) * width].astype(BF16)


def _act_projection(u, w, epilogues, cast_weights, *, name):
    m = u.shape[0]
    tm = ACT_TM
    n_steps = m // tm
    n_out = w.shape[1]
    assert sum(width for width, _ in epilogues) == n_out
    kern = functools.partial(_act_kernel, epilogues=tuple(epilogues), n_cast=len(cast_weights))

    cast_in_specs, cast_out_specs, cast_out_shapes = [], [], []
    for cw, width in cast_weights:
        rows = -(-cw.shape[0] // (n_steps * BF16_SUBLANES)) * BF16_SUBLANES
        while cw.shape[0] % rows:
            rows += BF16_SUBLANES
        last = cw.shape[0] // rows - 1

        def row_block(i, last=last):
            return jnp.minimum(i, last)

        cast_in_specs.append(pl.BlockSpec((rows, cw.shape[1]), lambda i, f=row_block: (f(i), 0)))
        if width is None:
            cast_out_specs.append(cast_in_specs[-1])
            cast_out_shapes.append(jax.ShapeDtypeStruct(cw.shape, BF16))
        else:
            n_blk = cw.shape[1] // width
            cast_out_specs.append(pl.BlockSpec(
                (n_blk, rows, width), lambda i, f=row_block: (0, f(i), 0)))
            cast_out_shapes.append(jax.ShapeDtypeStruct((n_blk, cw.shape[0], width), BF16))

    outs = pl.pallas_call(
        kern,
        grid=(n_steps,),
        in_specs=[
            pl.BlockSpec((tm, D_MODEL), lambda i: (i, 0)),
            pl.BlockSpec((D_MODEL, n_out), lambda i: (0, 0)),
        ] + cast_in_specs,
        out_specs=[pl.BlockSpec((tm, n_out), lambda i: (i, 0))] + cast_out_specs,
        out_shape=[jax.ShapeDtypeStruct((m, n_out), BF16)] + cast_out_shapes,
        compiler_params=_params_1d(),
        name=name,
    )(u, w, *[cw for cw, _ in cast_weights])
    return outs[0], outs[1:]


def _pool_kernel(u_ref, w_ref, wpg_ref, pscale_ref, o_ref, halo_ref, p_ref, a_ref, b_ref, *, bps):
    tm = u_ref.shape[0]
    hl = POOL_HALO
    n = tm + hl
    seq_block = pl.program_id(0) % bps
    pos = seq_block * tm + lax.broadcasted_iota(jnp.int32, (tm, 1), 0)
    zeros = jnp.zeros((hl, D_POOL), F32)
    p_ref[0:hl, :] = zeros
    a_ref[0:hl, :] = zeros
    b_ref[0:hl, :] = zeros
    for grp in reversed(range(len(POOL_WINDOWS))):
        win = POOL_WINDOWS[grp]
        cs = slice(grp * POOL_GROUP_DIM, (grp + 1) * POOL_GROUP_DIM)
        if grp % 2 == 1:
            ps = slice((grp - 1) * POOL_GROUP_DIM, (grp + 1) * POOL_GROUP_DIM)
            acc2 = jnp.dot(u_ref[...], w_ref[:, ps], preferred_element_type=F32)
            prev = halo_ref[:, ps]
            p_ref[hl:2 * hl, ps] = jnp.where(seq_block == 0, jnp.zeros_like(prev), prev)
            p_ref[2 * hl:2 * hl + tm, ps] = acc2
            halo_ref[:, ps] = acc2[tm - hl:, :]
        src = p_ref
        for lvl in range(win.bit_length() - 1):
            d = 1 << lvl
            dst = (a_ref, b_ref)[lvl % 2]
            dst[hl:hl + n, cs] = src[hl:hl + n, cs] + src[hl - d:hl - d + n, cs]
            src = dst
        s = src[2 * hl:2 * hl + tm, cs]
        inv_cnt = 1.0 / jnp.minimum(pos + 1, win).astype(F32)
        pooled = s * inv_cnt - p_ref[2 * hl:2 * hl + tm, cs]
        y = jnp.dot(pooled.astype(BF16), wpg_ref[grp], preferred_element_type=F32)
        o_ref[:, cs] = (y * pscale_ref[:, cs]).astype(BF16)


def _pool_projection(u, w_pool_cols, wpg, pscale, *, seq):
    m = u.shape[0]
    tm = PROJ_TM
    kern = functools.partial(_pool_kernel, bps=seq // tm)
    return pl.pallas_call(
        kern,
        grid=(m // tm,),
        in_specs=[
            pl.BlockSpec((tm, D_MODEL), lambda i: (i, 0)),
            pl.BlockSpec((D_MODEL, D_POOL), lambda i: (0, 0)),
            pl.BlockSpec((len(POOL_WINDOWS), POOL_GROUP_DIM, POOL_GROUP_DIM), lambda i: (0, 0, 0)),
            pl.BlockSpec((1, D_POOL), lambda i: (0, 0)),
        ],
        out_specs=pl.BlockSpec((tm, D_POOL), lambda i: (i, 0)),
        out_shape=jax.ShapeDtypeStruct((m, D_POOL), BF16),
        scratch_shapes=[pltpu.VMEM((POOL_HALO, D_POOL), F32)]
        + [pltpu.VMEM((tm + 2 * POOL_HALO, D_POOL), F32)] * 3,
        compiler_params=_params_1d(),
        name="pool_projection",
    )(u, w_pool_cols, wpg, pscale)


def _mixer_kernel(q_ref, k_ref, v_ref, rg_ref, dmask_ref, bdec_ref, pool_ref, gate_ref, x_ref,
                  wr_ref, wp_ref, wo_ref, o_ref, state_ref, og_ref, m_ref):
    @pl.when(pl.program_id(1) == 0)
    def _():
        state_ref[...] = jnp.zeros_like(state_ref)

    for h in range(N_HEADS):
        cs = slice(h * HEAD_DIM, (h + 1) * HEAD_DIM)
        q = q_ref[:, cs]
        k = k_ref[:, cs]
        v = v_ref[:, cs]
        scores = lax.dot_general(q, k, (((1,), (1,)), ((), ())), preferred_element_type=F32)
        p = (scores * dmask_ref[h]).astype(BF16)
        st = state_ref[h]
        o = (jnp.dot(p, v, preferred_element_type=F32)
             + jnp.dot(q, st.astype(BF16), preferred_element_type=F32))
        upd = lax.dot_general(k, v, (((0,), (0,)), ((), ())), preferred_element_type=F32)
        state_ref[h] = st * bdec_ref[h] + upd
        og_ref[:, cs] = (_rms_scale(o) * rg_ref[:, cs].astype(F32)).astype(BF16)

    og = og_ref[...]
    pool = pool_ref[...]
    for c0 in range(0, D_MODEL, MERGE_PIECE):
        cs = slice(c0, c0 + MERGE_PIECE)
        y_ret = jnp.dot(og, wr_ref[:, cs], preferred_element_type=F32)
        y_pool = jnp.dot(pool, wp_ref[:, cs], preferred_element_type=F32)
        g_ret = gate_ref[:, cs].astype(F32)
        g_pool = gate_ref[:, D_MODEL + c0:D_MODEL + c0 + MERGE_PIECE].astype(F32)
        m_ref[:, cs] = (g_ret * y_ret + g_pool * y_pool).astype(BF16)
    merged = m_ref[...]
    for c0 in range(0, D_MODEL, MERGE_PIECE):
        cs = slice(c0, c0 + MERGE_PIECE)
        o_ref[:, cs] = x_ref[:, cs] + jnp.dot(merged, wo_ref[:, cs], preferred_element_type=F32)


def _mixers(qk, vr, gates, pooled, x2d, dmask, bdec, w_ret, w_pool, w_out, *, batch, seq):
    blk = RET_BLOCK
    nblk = seq // blk

    def rows(width, c=0):
        return pl.BlockSpec((blk, width), lambda b, t: (b * nblk + t, c))

    def whole(shape):
        return pl.BlockSpec(shape, lambda b, t: (0,) * len(shape))

    return pl.pallas_call(
        _mixer_kernel,
        grid=(batch, nblk),
        in_specs=[
            rows(D_RET, 0), rows(D_RET, 1),
            rows(D_RET, 0), rows(D_RET, 1),
            whole((N_HEADS, blk, blk)), whole((N_HEADS, 1, HEAD_DIM)),
            rows(D_POOL), rows(2 * D_MODEL),
            rows(D_MODEL),
            whole((D_RET, D_MODEL)), whole((D_POOL, D_MODEL)), whole((D_MODEL, D_MODEL)),
        ],
        out_specs=rows(D_MODEL),
        out_shape=jax.ShapeDtypeStruct((batch * seq, D_MODEL), F32),
        scratch_shapes=[
            pltpu.VMEM((N_HEADS, HEAD_DIM, HEAD_DIM), F32),
            pltpu.VMEM((blk, D_RET), BF16),
            pltpu.VMEM((blk, D_MODEL), BF16),
        ],
        compiler_params=_params(),
        name="mixers",
    )(qk, qk, vr, vr, dmask, bdec, pooled, gates, x2d, w_ret, w_pool, w_out)


def _retention_tables():
    blk = RET_BLOCK
    f32 = np.float32
    log_g = np.log(f32(1.0) - f32(2.0) ** (f32(-5.0) - np.arange(N_HEADS, dtype=f32)))
    n = np.arange(blk, dtype=f32)
    diff = n[:, None] - n[None, :]
    chunk = np.arange(blk) // CHUNK
    visible = (chunk[None, :] <= chunk[:, None]).astype(f32)
    dmask = np.exp(log_g[:, None, None] * (np.abs(diff) - (f32(blk) + diff))[None]) * visible[None]
    qdec = np.exp(log_g[:, None] * (n[None, :] + f32(1.0)))
    kdec = np.exp(log_g[:, None] * (f32(blk - 1.0) - n[None, :]))
    row_decay = np.concatenate([qdec, kdec], axis=0)
    row_decay = np.broadcast_to(row_decay[:, :, None], (2 * N_HEADS, blk, HEAD_DIM // 2))
    bdec = np.exp(log_g * f32(blk))
    bdec = np.broadcast_to(bdec[:, None, None], (N_HEADS, 1, HEAD_DIM))
    return dmask, row_decay, bdec


def _ffn_kernel(h_ref, g2_ref, wa_ref, wb_ref, wo_ref, gf_ref, o_ref, u_ref, *, n_ff):
    j = pl.program_id(1)
    tm = o_ref.shape[0]

    def hidden_block(u, base):
        a = jnp.dot(u, wa_ref[...], preferred_element_type=F32)
        b = jnp.dot(u, wb_ref[...], preferred_element_type=F32)
        gated = (a * _sigmoid(a) * b).astype(BF16)
        return base + jnp.dot(gated, wo_ref[...], preferred_element_type=F32)

    @pl.when(j == 0)
    def _():
        for r0 in range(0, tm, FFN_NORM_ROWS):
            rs = slice(r0, r0 + FFN_NORM_ROWS)
            h = h_ref[rs, :]
            u = (_rms_scale(h) * g2_ref[...]).astype(BF16)
            u_ref[rs, :] = u
            o_ref[rs, :] = hidden_block(u, h)

    @pl.when(jnp.logical_and(j > 0, j < n_ff - 1))
    def _():
        o_ref[...] = hidden_block(u_ref[...], o_ref[...])

    @pl.when(j == n_ff - 1)
    def _():
        for r0 in range(0, tm, FFN_NORM_ROWS):
            rs = slice(r0, r0 + FFN_NORM_ROWS)
            h2 = hidden_block(u_ref[rs, :], o_ref[rs, :])
            o_ref[rs, :] = _rms_scale(h2) * gf_ref[...]


def _ffn(h1, g2, w_ffn_in, w_ffn_out, g_final, *, tm=FFN_TM, tf=FFN_TF):
    m = h1.shape[0]
    assert w_ffn_in.shape == (2 * D_FF // tf, D_MODEL, tf)
    n_ff = D_FF // tf
    assert n_ff >= 2
    kern = functools.partial(_ffn_kernel, n_ff=n_ff)
    return pl.pallas_call(
        kern,
        grid=(m // tm, n_ff),
        in_specs=[
            pl.BlockSpec((tm, D_MODEL), lambda i, j: (i, 0)),
            pl.BlockSpec((1, D_MODEL), lambda i, j: (0, 0)),
            pl.BlockSpec((None, D_MODEL, tf), lambda i, j: (j, 0, 0)),
            pl.BlockSpec((None, D_MODEL, tf), lambda i, j: (n_ff + j, 0, 0)),
            pl.BlockSpec((tf, D_MODEL), lambda i, j: (j, 0)),
            pl.BlockSpec((1, D_MODEL), lambda i, j: (0, 0)),
        ],
        out_specs=pl.BlockSpec((tm, D_MODEL), lambda i, j: (i, 0)),
        out_shape=jax.ShapeDtypeStruct((m, D_MODEL), F32),
        scratch_shapes=[pltpu.VMEM((tm, D_MODEL), BF16)],
        compiler_params=_params(),
        name="swiglu_ffn",
    )(h1, g2, w_ffn_in, w_ffn_in, w_ffn_out, g_final)


def _rotary_tables(seq):
    f32 = np.float32
    inv_freq = f32(1.0) / (f32(ROPE_BASE) ** (np.arange(0, HEAD_DIM, 2, dtype=f32) / f32(HEAD_DIM)))
    ang = np.arange(seq).astype(f32)[:, None] * inv_freq[None, :]
    scale = f32(HEAD_DIM ** -0.5)
    return np.cos(ang) * scale, np.sin(ang) * scale


def kernel(x, norm1_g, w_in, w_ret_branch, w_pool_group, pool_scale, w_pool_branch, w_out,
           norm2_g, w_ffn_in, w_ffn_out, norm_final_g):
    batch, seq, d = x.shape
    assert d == D_MODEL and norm1_g.shape[0] == 1
    assert seq % PROJ_TM == 0 and seq % QK_TM == 0 and seq % ACT_TM == 0
    assert math.log2(HEAD_DIM ** -0.5) == -4.0
    x2d = x.reshape(batch * seq, D_MODEL)
    cos_t, sin_t = _rotary_tables(seq)
    dmask, row_decay, bdec = _retention_tables()
    w_qk = w_in[0, :, :2 * D_RET].astype(BF16)

    qk, u, w_vr, w_pool_cols, w_gate = _qk_projection(
        x2d, norm1_g[0][None, :], w_qk, w_in[0], cos_t, sin_t, row_decay, seq=seq)
    vr, (w_ret_b, w_pool_b, w_out_b, w_ffn_out_b) = _act_projection(
        u, w_vr, ((D_RET, lambda a: a), (D_RET, lambda a: a * _sigmoid(a))),
        ((w_ret_branch[0], None), (w_pool_branch[0], None), (w_out[0], None), (w_ffn_out[0], None)),
        name="v_rg_projection")
    gates, (w_ffn_in_b,) = _act_projection(
        u, w_gate, ((2 * D_MODEL, _sigmoid),), ((w_ffn_in[0], FFN_TF),), name="gate_projection")
    pooled = _pool_projection(u, w_pool_cols, w_pool_group[0].astype(BF16), pool_scale[0][None, :],
                              seq=seq)
    h1 = _mixers(qk, vr, gates, pooled, x2d, dmask, bdec, w_ret_b, w_pool_b, w_out_b,
                 batch=batch, seq=seq)
    out = _ffn(h1, norm2_g[0][None, :], w_ffn_in_b, w_ffn_out_b, norm_final_g[None, :])
    return out.reshape(batch, seq, D_MODEL)
```

```python
import functools
import math

import jax
import jax.numpy as jnp
import numpy as np
from jax import lax
from jax.experimental import pallas as pl
from jax.experimental.pallas import tpu as pltpu

D_MODEL = 2048
CHUNK = 64
N_HEADS = 8
HEAD_DIM = 256
D_RET = N_HEADS * HEAD_DIM
POOL_WINDOWS = (2, 4, 8, 16)
POOL_GROUP_DIM = 256
D_POOL = 1024
D_FF = 5632
D_PROJ = 4 * D_RET + D_POOL + 2 * D_MODEL
ROPE_BASE = 10000.0
NORM_EPS = 1e-6
POOL_HALO = 16
assert all(w & (w - 1) == 0 and w <= POOL_HALO for w in POOL_WINDOWS)

VMEM_LIMIT_BYTES = 56 * 1024 * 1024
RET_BLOCK = 256
QK_TM = 512
ACT_TM = 1024
PROJ_TM = 1024
PROJ_PIECE = 256
NORM_ROWS = 256
FFN_TM = 1024
FFN_TF = 512
FFN_NORM_ROWS = 512
MERGE_PIECE = 512

F32 = jnp.float32
BF16 = jnp.bfloat16
BF16_SUBLANES = 16


def _rms_scale(xf):
    return xf * lax.rsqrt(jnp.mean(xf * xf, axis=-1, keepdims=True) + NORM_EPS)


def _sigmoid(x):
    return 0.5 * jnp.tanh(0.5 * x) + 0.5


def _params():
    return pltpu.CompilerParams(
        dimension_semantics=("arbitrary", "arbitrary"),
        vmem_limit_bytes=VMEM_LIMIT_BYTES)


def _params_1d():
    return pltpu.CompilerParams(
        dimension_semantics=("arbitrary",), vmem_limit_bytes=VMEM_LIMIT_BYTES)


def _qk_kernel(x_ref, g_ref, w_ref, cos_ref, sin_ref, dec_ref, wsrc_ref, o_ref, uo_ref, *wdst_refs):
    half = HEAD_DIM // 2
    assert RET_BLOCK == NORM_ROWS
    for r0 in range(0, o_ref.shape[0], NORM_ROWS):
        rs = slice(r0, r0 + NORM_ROWS)
        u = (_rms_scale(x_ref[rs, :]) * g_ref[...]).astype(BF16)
        uo_ref[rs, :] = u
        cos = cos_ref[rs, :]
        sin = sin_ref[rs, :]
        for hh in range(o_ref.shape[1] // HEAD_DIM):
            c0 = hh * HEAD_DIM
            acc = jnp.dot(u, w_ref[:, c0:c0 + HEAD_DIM], preferred_element_type=F32)
            t1 = acc[:, :half]
            t2 = acc[:, half:]
            dec = dec_ref[hh]
            o_ref[rs, c0:c0 + half] = ((t1 * cos - t2 * sin) * dec).astype(BF16)
            o_ref[rs, c0 + half:c0 + HEAD_DIM] = ((t1 * sin + t2 * cos) * dec).astype(BF16)

    c0 = o_ref.shape[1]
    for wdst_ref in wdst_refs:
        width = wdst_ref.shape[1]
        wdst_ref[...] = wsrc_ref[:, c0:c0 + width].astype(BF16)
        c0 += width


def _qk_projection(x2d, g1, w_qk, w_in, cos_t, sin_t, row_decay, *, seq):
    m = x2d.shape[0]
    tm = QK_TM
    bps = seq // tm
    rest_widths = (D_RET, D_RET, D_POOL, D_MODEL, D_MODEL)
    assert 2 * D_RET + sum(rest_widths) == D_PROJ
    cast_rows = D_MODEL // (m // tm)
    assert cast_rows % BF16_SUBLANES == 0
    return pl.pallas_call(
        _qk_kernel,
        grid=(m // tm,),
        in_specs=[
            pl.BlockSpec((tm, D_MODEL), lambda i: (i, 0)),
            pl.BlockSpec((1, D_MODEL), lambda i: (0, 0)),
            pl.BlockSpec((D_MODEL, 2 * D_RET), lambda i: (0, 0)),
            pl.BlockSpec((tm, HEAD_DIM // 2), lambda i: (i % bps, 0)),
            pl.BlockSpec((tm, HEAD_DIM // 2), lambda i: (i % bps, 0)),
            pl.BlockSpec((2 * N_HEADS, RET_BLOCK, HEAD_DIM // 2), lambda i: (0, 0, 0)),
            pl.BlockSpec((cast_rows, D_PROJ), lambda i: (i, 0)),
        ],
        out_specs=[
            pl.BlockSpec((tm, 2 * D_RET), lambda i: (i, 0)),
            pl.BlockSpec((tm, D_MODEL), lambda i: (i, 0)),
        ] + [pl.BlockSpec((cast_rows, width), lambda i: (i, 0)) for width in rest_widths],
        out_shape=[
            jax.ShapeDtypeStruct((m, 2 * D_RET), BF16),
            jax.ShapeDtypeStruct((m, D_MODEL), BF16),
        ] + [jax.ShapeDtypeStruct((D_MODEL, width), BF16) for width in rest_widths],
        compiler_params=_params_1d(),
        name="qk_projection",
    )(x2d, g1, w_qk, cos_t, sin_t, row_decay, w_in)


def _act_kernel(u_ref, w_ref, *refs, epilogue, n_cast):
    cast_in = refs[:n_cast]
    o_ref = refs[n_cast]
    cast_out = refs[n_cast + 1:]
    for c0 in range(0, o_ref.shape[1], PROJ_PIECE):
        cs = slice(c0, c0 + PROJ_PIECE)
        acc = jnp.dot(u_ref[...], w_ref[:, cs], preferred_element_type=F32)
        o_ref[:, cs] = epilogue(acc).astype(BF16)
    for src, dst in zip(cast_in, cast_out):
        if len(dst.shape) == 2:
            dst[...] = src[...].astype(BF16)
        else:
            width = dst.shape[2]
            for c in range(dst.shape[0]):
                dst[c] = src[:, c * width:(c + 1) * width].astype(BF16)


def _act_projection(u, w, epilogue, cast_weights, *, name):
    m = u.shape[0]
    tm = ACT_TM
    n_steps = m // tm
    n_out = w.shape[1]
    kern = functools.partial(_act_kernel, epilogue=epilogue, n_cast=len(cast_weights))

    cast_in_specs, cast_out_specs, cast_out_shapes = [], [], []
    for cw, width, window in cast_weights:
        k, n_win = window or (0, 1)
        n_rows, n_cols = cw.shape[0], cw.shape[1] // n_win
        rows = -(-n_rows // (n_steps * BF16_SUBLANES)) * BF16_SUBLANES
        while n_rows % rows:
            rows += BF16_SUBLANES
        last = n_rows // rows - 1

        def row_block(i, last=last):
            return jnp.minimum(i, last)

        cast_in_specs.append(pl.BlockSpec((rows, n_cols), lambda i, f=row_block, k=k: (f(i), k)))
        if width is None:
            cast_out_specs.append(pl.BlockSpec((rows, n_cols), lambda i, f=row_block: (f(i), 0)))
            cast_out_shapes.append(jax.ShapeDtypeStruct((n_rows, n_cols), BF16))
        else:
            n_blk = n_cols // width
            cast_out_specs.append(pl.BlockSpec(
                (n_blk, rows, width), lambda i, f=row_block: (0, f(i), 0)))
            cast_out_shapes.append(jax.ShapeDtypeStruct((n_blk, n_rows, width), BF16))

    outs = pl.pallas_call(
        kern,
        grid=(n_steps,),
        in_specs=[
            pl.BlockSpec((tm, D_MODEL), lambda i: (i, 0)),
            pl.BlockSpec((D_MODEL, n_out), lambda i: (0, 0)),
        ] + cast_in_specs,
        out_specs=[pl.BlockSpec((tm, n_out), lambda i: (i, 0))] + cast_out_specs,
        out_shape=[jax.ShapeDtypeStruct((m, n_out), BF16)] + cast_out_shapes,
        compiler_params=_params_1d(),
        name=name,
    )(u, w, *[cw for cw, _, _ in cast_weights])
    return outs[0], outs[1:]


def _pool_kernel(u_ref, w_ref, wpg_ref, pscale_ref, o_ref, halo_ref, p_ref, a_ref, b_ref, *, bps):
    tm = u_ref.shape[0]
    hl = POOL_HALO
    n = tm + hl
    seq_block = pl.program_id(0) % bps
    pos = seq_block * tm + lax.broadcasted_iota(jnp.int32, (tm, 1), 0)
    zeros = jnp.zeros((hl, D_POOL), F32)
    p_ref[0:hl, :] = zeros
    a_ref[0:hl, :] = zeros
    b_ref[0:hl, :] = zeros
    for grp in reversed(range(len(POOL_WINDOWS))):
        win = POOL_WINDOWS[grp]
        cs = slice(grp * POOL_GROUP_DIM, (grp + 1) * POOL_GROUP_DIM)
        if grp % 2 == 1:
            ps = slice((grp - 1) * POOL_GROUP_DIM, (grp + 1) * POOL_GROUP_DIM)
            acc2 = jnp.dot(u_ref[...], w_ref[:, ps], preferred_element_type=F32)
            prev = halo_ref[:, ps]
            p_ref[hl:2 * hl, ps] = jnp.where(seq_block == 0, jnp.zeros_like(prev), prev)
            p_ref[2 * hl:2 * hl + tm, ps] = acc2
            halo_ref[:, ps] = acc2[tm - hl:, :]
        src = p_ref
        for lvl in range(win.bit_length() - 1):
            d = 1 << lvl
            dst = (a_ref, b_ref)[lvl % 2]
            dst[hl:hl + n, cs] = src[hl:hl + n, cs] + src[hl - d:hl - d + n, cs]
            src = dst
        s = src[2 * hl:2 * hl + tm, cs]
        inv_cnt = 1.0 / jnp.minimum(pos + 1, win).astype(F32)
        pooled = s * inv_cnt - p_ref[2 * hl:2 * hl + tm, cs]
        y = jnp.dot(pooled.astype(BF16), wpg_ref[grp], preferred_element_type=F32)
        o_ref[:, cs] = (y * pscale_ref[:, cs]).astype(BF16)


def _pool_projection(u, w_pool_cols, wpg, pscale, *, seq):
    m = u.shape[0]
    tm = PROJ_TM
    kern = functools.partial(_pool_kernel, bps=seq // tm)
    return pl.pallas_call(
        kern,
        grid=(m // tm,),
        in_specs=[
            pl.BlockSpec((tm, D_MODEL), lambda i: (i, 0)),
            pl.BlockSpec((D_MODEL, D_POOL), lambda i: (0, 0)),
            pl.BlockSpec((len(POOL_WINDOWS), POOL_GROUP_DIM, POOL_GROUP_DIM), lambda i: (0, 0, 0)),
            pl.BlockSpec((1, D_POOL), lambda i: (0, 0)),
        ],
        out_specs=pl.BlockSpec((tm, D_POOL), lambda i: (i, 0)),
        out_shape=jax.ShapeDtypeStruct((m, D_POOL), BF16),
        scratch_shapes=[pltpu.VMEM((POOL_HALO, D_POOL), F32)]
        + [pltpu.VMEM((tm + 2 * POOL_HALO, D_POOL), F32)] * 3,
        compiler_params=_params_1d(),
        name="pool_projection",
    )(u, w_pool_cols, wpg, pscale)


def _mixer_kernel(q_ref, k_ref, v_ref, rg_ref, dmask_ref, bdec_ref, pool_ref, gr_ref, gp_ref, x_ref,
                  wr_ref, wp_ref, wo_ref, o_ref, state_ref, og_ref, m_ref):
    @pl.when(pl.program_id(1) == 0)
    def _():
        state_ref[...] = jnp.zeros_like(state_ref)

    for h in range(N_HEADS):
        cs = slice(h * HEAD_DIM, (h + 1) * HEAD_DIM)
        q = q_ref[:, cs]
        k = k_ref[:, cs]
        v = v_ref[:, cs]
        scores = lax.dot_general(q, k, (((1,), (1,)), ((), ())), preferred_element_type=F32)
        p = (scores * dmask_ref[h]).astype(BF16)
        st = state_ref[h]
        o = (jnp.dot(p, v, preferred_element_type=F32)
             + jnp.dot(q, st.astype(BF16), preferred_element_type=F32))
        upd = lax.dot_general(k, v, (((0,), (0,)), ((), ())), preferred_element_type=F32)
        state_ref[h] = st * bdec_ref[h] + upd
        og_ref[:, cs] = (_rms_scale(o) * rg_ref[:, cs].astype(F32)).astype(BF16)

    og = og_ref[...]
    pool = pool_ref[...]
    for c0 in range(0, D_MODEL, MERGE_PIECE):
        cs = slice(c0, c0 + MERGE_PIECE)
        y_ret = jnp.dot(og, wr_ref[:, cs], preferred_element_type=F32)
        y_pool = jnp.dot(pool, wp_ref[:, cs], preferred_element_type=F32)
        g_ret = gr_ref[:, cs].astype(F32)
        g_pool = gp_ref[:, cs].astype(F32)
        m_ref[:, cs] = (g_ret * y_ret + g_pool * y_pool).astype(BF16)
    merged = m_ref[...]
    for c0 in range(0, D_MODEL, MERGE_PIECE):
        cs = slice(c0, c0 + MERGE_PIECE)
        o_ref[:, cs] = x_ref[:, cs] + jnp.dot(merged, wo_ref[:, cs], preferred_element_type=F32)


def _mixers(qk, v, srg, g_ret, g_pool, pooled, x2d, dmask, bdec, w_ret, w_pool, w_out, *, batch,
            seq):
    blk = RET_BLOCK
    nblk = seq // blk

    def rows(width, c=0):
        return pl.BlockSpec((blk, width), lambda b, t: (b * nblk + t, c))

    def whole(shape):
        return pl.BlockSpec(shape, lambda b, t: (0,) * len(shape))

    return pl.pallas_call(
        _mixer_kernel,
        grid=(batch, nblk),
        in_specs=[
            rows(D_RET, 0), rows(D_RET, 1),
            rows(D_RET), rows(D_RET),
            whole((N_HEADS, blk, blk)), whole((N_HEADS, 1, HEAD_DIM)),
            rows(D_POOL), rows(D_MODEL), rows(D_MODEL),
            rows(D_MODEL),
            whole((D_RET, D_MODEL)), whole((D_POOL, D_MODEL)), whole((D_MODEL, D_MODEL)),
        ],
        out_specs=rows(D_MODEL),
        out_shape=jax.ShapeDtypeStruct((batch * seq, D_MODEL), F32),
        scratch_shapes=[
            pltpu.VMEM((N_HEADS, HEAD_DIM, HEAD_DIM), F32),
            pltpu.VMEM((blk, D_RET), BF16),
            pltpu.VMEM((blk, D_MODEL), BF16),
        ],
        compiler_params=_params(),
        name="mixers",
    )(qk, qk, v, srg, dmask, bdec, pooled, g_ret, g_pool, x2d, w_ret, w_pool, w_out)


def _retention_tables():
    blk = RET_BLOCK
    f32 = np.float32
    log_g = np.log(f32(1.0) - f32(2.0) ** (f32(-5.0) - np.arange(N_HEADS, dtype=f32)))
    n = np.arange(blk, dtype=f32)
    diff = n[:, None] - n[None, :]
    chunk = np.arange(blk) // CHUNK
    visible = (chunk[None, :] <= chunk[:, None]).astype(f32)
    dmask = np.exp(log_g[:, None, None] * (np.abs(diff) - (f32(blk) + diff))[None]) * visible[None]
    qdec = np.exp(log_g[:, None] * (n[None, :] + f32(1.0)))
    kdec = np.exp(log_g[:, None] * (f32(blk - 1.0) - n[None, :]))
    row_decay = np.concatenate([qdec, kdec], axis=0)
    row_decay = np.broadcast_to(row_decay[:, :, None], (2 * N_HEADS, blk, HEAD_DIM // 2))
    bdec = np.exp(log_g * f32(blk))
    bdec = np.broadcast_to(bdec[:, None, None], (N_HEADS, 1, HEAD_DIM))
    return dmask, row_decay, bdec


def _ffn_kernel(h_ref, g2_ref, wa_ref, wb_ref, wo_ref, gf_ref, o_ref, u_ref, *, n_ff):
    j = pl.program_id(1)
    tm = o_ref.shape[0]

    def hidden_block(u, base):
        a = jnp.dot(u, wa_ref[...], preferred_element_type=F32)
        b = jnp.dot(u, wb_ref[...], preferred_element_type=F32)
        gated = (a * _sigmoid(a) * b).astype(BF16)
        return base + jnp.dot(gated, wo_ref[...], preferred_element_type=F32)

    @pl.when(j == 0)
    def _():
        for r0 in range(0, tm, FFN_NORM_ROWS):
            rs = slice(r0, r0 + FFN_NORM_ROWS)
            h = h_ref[rs, :]
            u = (_rms_scale(h) * g2_ref[...]).astype(BF16)
            u_ref[rs, :] = u
            o_ref[rs, :] = hidden_block(u, h)

    @pl.when(jnp.logical_and(j > 0, j < n_ff - 1))
    def _():
        o_ref[...] = hidden_block(u_ref[...], o_ref[...])

    @pl.when(j == n_ff - 1)
    def _():
        for r0 in range(0, tm, FFN_NORM_ROWS):
            rs = slice(r0, r0 + FFN_NORM_ROWS)
            h2 = hidden_block(u_ref[rs, :], o_ref[rs, :])
            o_ref[rs, :] = _rms_scale(h2) * gf_ref[...]


def _ffn(h1, g2, w_a, w_b, w_ffn_out, g_final, *, tm=FFN_TM, tf=FFN_TF):
    m = h1.shape[0]
    assert w_a.shape == w_b.shape == (D_FF // tf, D_MODEL, tf)
    n_ff = D_FF // tf
    assert n_ff >= 2
    kern = functools.partial(_ffn_kernel, n_ff=n_ff)
    return pl.pallas_call(
        kern,
        grid=(m // tm, n_ff),
        in_specs=[
            pl.BlockSpec((tm, D_MODEL), lambda i, j: (i, 0)),
            pl.BlockSpec((1, D_MODEL), lambda i, j: (0, 0)),
            pl.BlockSpec((None, D_MODEL, tf), lambda i, j: (j, 0, 0)),
            pl.BlockSpec((None, D_MODEL, tf), lambda i, j: (j, 0, 0)),
            pl.BlockSpec((tf, D_MODEL), lambda i, j: (j, 0)),
            pl.BlockSpec((1, D_MODEL), lambda i, j: (0, 0)),
        ],
        out_specs=pl.BlockSpec((tm, D_MODEL), lambda i, j: (i, 0)),
        out_shape=jax.ShapeDtypeStruct((m, D_MODEL), F32),
        scratch_shapes=[pltpu.VMEM((tm, D_MODEL), BF16)],
        compiler_params=_params(),
        name="swiglu_ffn",
    )(h1, g2, w_a, w_b, w_ffn_out, g_final)


def _rotary_tables(seq):
    f32 = np.float32
    inv_freq = f32(1.0) / (f32(ROPE_BASE) ** (np.arange(0, HEAD_DIM, 2, dtype=f32) / f32(HEAD_DIM)))
    ang = np.arange(seq).astype(f32)[:, None] * inv_freq[None, :]
    scale = f32(HEAD_DIM ** -0.5)
    return np.cos(ang) * scale, np.sin(ang) * scale


def kernel(x, norm1_g, w_in, w_ret_branch, w_pool_group, pool_scale, w_pool_branch, w_out,
           norm2_g, w_ffn_in, w_ffn_out, norm_final_g):
    batch, seq, d = x.shape
    assert d == D_MODEL and norm1_g.shape[0] == 1
    assert seq % PROJ_TM == 0 and seq % QK_TM == 0 and seq % ACT_TM == 0
    assert math.log2(HEAD_DIM ** -0.5) == -4.0
    x2d = x.reshape(batch * seq, D_MODEL)
    cos_t, sin_t = _rotary_tables(seq)
    dmask, row_decay, bdec = _retention_tables()
    w_qk = w_in[0, :, :2 * D_RET].astype(BF16)

    qk, u, w_v, w_rg, w_pool_cols, w_gr, w_gp = _qk_projection(
        x2d, norm1_g[0][None, :], w_qk, w_in[0], cos_t, sin_t, row_decay, seq=seq)
    v, (w_ret_b, w_pool_b, w_out_b) = _act_projection(
        u, w_v, lambda a: a,
        ((w_ret_branch[0], None, None), (w_pool_branch[0], None, None), (w_out[0], None, None)),
        name="v_projection")
    srg, (w_ffn_out_b,) = _act_projection(
        u, w_rg, lambda a: a * _sigmoid(a), ((w_ffn_out[0], None, None),), name="rg_projection")
    g_ret, (w_ffn_a,) = _act_projection(
        u, w_gr, _sigmoid, ((w_ffn_in[0], FFN_TF, (0, 2)),), name="g_ret_projection")
    g_pool, (w_ffn_b,) = _act_projection(
        u, w_gp, _sigmoid, ((w_ffn_in[0], FFN_TF, (1, 2)),), name="g_pool_projection")
    pooled = _pool_projection(u, w_pool_cols, w_pool_group[0].astype(BF16), pool_scale[0][None, :],
                              seq=seq)
    h1 = _mixers(qk, v, srg, g_ret, g_pool, pooled, x2d, dmask, bdec, w_ret_b, w_pool_b, w_out_b,
                 batch=batch, seq=seq)
    out = _ffn(h1, norm2_g[0][None, :], w_ffn_a, w_ffn_b, w_ffn_out_b, norm_final_g[None, :])
    return out.reshape(batch, seq, D_MODEL)
```
